```python
import math
import jax
import jax.numpy as jnp
from jax import lax
import numpy as np

D_MODEL = 1024
BATCH = 16
SEQ = 4096
DEPTH = 4

CTX_LEN = 256
GRID_W = 64
SSM_WIDTH = D_MODEL // 2
CONV_WIDTH = D_MODEL // 2
EVEN_PROJ = SSM_WIDTH + 3 * CONV_WIDTH
SSM_GROUP = 16
SSM_GROUPS = SSM_WIDTH // SSM_GROUP
SSM_STATE = 64
STEP_MIN = 1e-3
STEP_MAX = 1e-1
CONV_K = 3
POOL_WINDOWS = (2, 4, 8, 16)
POOL_GROUP = D_MODEL // len(POOL_WINDOWS)
PEER_HEADS = 8
PEER_NKEYS = 128
PEER_EXPERTS = PEER_NKEYS * PEER_NKEYS
PEER_QDIM = 256
PEER_HALF = PEER_QDIM // 2
PEER_TOPK = 16
TOK_BLOCK = 128
LN_EPS = 1e-5
DN_ALPHA = (2.0 * DEPTH) ** 0.25
DN_BETA = (8.0 * DEPTH) ** -0.25
N_EVEN = (DEPTH + 1) // 2
N_ODD = DEPTH // 2

kernel_name = 'hybrid_s5_conv_pool_peer_diffusion'


def layer_norm(h, g, b):
    hf = h.astype(jnp.float32)
    mu = jnp.mean(hf, axis=-1, keepdims=True)
    var = jnp.mean(jnp.square(hf - mu), axis=-1, keepdims=True)
    return ((hf - mu) * lax.rsqrt(var + LN_EPS) * g + b).astype(h.dtype)


def s5_discretise(lam_re, lam_im, log_step, b_re, b_im):
    f32 = jnp.float32
    step = jnp.exp(log_step.astype(f32))[:, None]
    lr = lam_re.astype(f32)
    li = lam_im.astype(f32)
    ld_re = lr * step
    ld_im = li * step
    mag = jnp.exp(ld_re)
    ab_re = mag * jnp.cos(ld_im)
    ab_im = mag * jnp.sin(ld_im)
    den = lr * lr + li * li
    nr = ab_re - 1.0
    f_re = (nr * lr + ab_im * li) / den
    f_im = (ab_im * lr - nr * li) / den
    br = b_re.astype(f32)
    bi = b_im.astype(f32)
    bb_re = f_re[..., None] * br - f_im[..., None] * bi
    bb_im = f_re[..., None] * bi + f_im[..., None] * br
    return ab_re, ab_im, bb_re, bb_im, ld_re, ld_im


def _complex_affine_combine(left, right):
    a_re, a_im, b_re, b_im = left
    c_re, c_im, d_re, d_im = right
    return (c_re * a_re - c_im * a_im,
            c_re * a_im + c_im * a_re,
            c_re * b_re - c_im * b_im + d_re,
            c_re * b_im + c_im * b_re + d_im)


def s5_bidir(u, lam_re, lam_im, log_step, b_re, b_im, c_re, c_im, d_skip, h0s, with_output):
    f32 = jnp.float32
    bsz, n, _ = u.shape
    u32 = u.astype(f32).reshape(bsz, n, SSM_GROUPS, SSM_GROUP)
    y = d_skip.astype(f32) * u32.reshape(bsz, n, SSM_WIDTH) if with_output else None
    finals = []
    for direction in range(2):
        reverse = direction == 1
        ab_re, ab_im, bb_re, bb_im, ld_re, ld_im = s5_discretise(
            lam_re[direction], lam_im[direction], log_step[direction], b_re[direction], b_im[direction])
        bu_re = jnp.einsum('blgi,gpi->blgp', u32, bb_re)
        bu_im = jnp.einsum('blgi,gpi->blgp', u32, bb_im)
        if reverse:
            bu_re = jnp.flip(bu_re, axis=1)
            bu_im = jnp.flip(bu_im, axis=1)
        a_re = jnp.broadcast_to(ab_re, (1, n) + ab_re.shape)
        a_im = jnp.broadcast_to(ab_im, (1, n) + ab_im.shape)
        _, _, h_re, h_im = lax.associative_scan(
            _complex_affine_combine, (a_re, a_im, bu_re, bu_im), axis=1)
        if h0s is not None:
            k = jnp.arange(1, n + 1, dtype=f32)[:, None, None]
            pmag = jnp.exp(k * ld_re)
            p_re = pmag * jnp.cos(k * ld_im)
            p_im = pmag * jnp.sin(k * ld_im)
            h0_re = h0s[direction][0][:, None]
            h0_im = h0s[direction][1][:, None]
            h_re, h_im = (h_re + p_re * h0_re - p_im * h0_im,
                          h_im + p_re * h0_im + p_im * h0_re)
        finals.append((h_re[:, -1], h_im[:, -1]))
        if with_output:
            if reverse:
                h_re = jnp.flip(h_re, axis=1)
                h_im = jnp.flip(h_im, axis=1)
            yc = (jnp.einsum('blgp,gip->blgi', h_re, c_re[direction].astype(f32))
                  - jnp.einsum('blgp,gip->blgi', h_im, c_im[direction].astype(f32)))
            y = y + yc.reshape(bsz, n, SSM_WIDTH)
    if with_output:
        y = y.astype(u.dtype)
    return y, finals


def gated_short_conv(p, conv_w, conv_b, n_seg, seg_len):
    bsz = p.shape[0]
    b_gate, c_gate, v = jnp.split(p, 3, axis=-1)
    z = (c_gate * v).reshape(bsz * n_seg, seg_len, CONV_WIDTH)
    z = lax.conv_general_dilated(
        z, conv_w[:, None, :], window_strides=(1,),
        padding=((CONV_K // 2, CONV_K // 2),),
        dimension_numbers=('NWC', 'WIO', 'NWC'),
        feature_group_count=CONV_WIDTH) + conv_b
    return b_gate * z.reshape(bsz, n_seg * seg_len, CONV_WIDTH)


def even_mixer(a, w_in, w_out, ssm, glu_w, glu_b, conv_w, conv_b, n_seg, seg_len, h0s):
    p = jnp.einsum('bld,de->ble', a, w_in)
    y_ssm, finals = s5_bidir(p[..., :SSM_WIDTH], *ssm, h0s=h0s, with_output=True)
    g = jax.nn.gelu(y_ssm, approximate=False)
    y_a = g * jax.nn.sigmoid(g @ glu_w + glu_b)
    y_b = gated_short_conv(p[..., SSM_WIDTH:], conv_w, conv_b, n_seg, seg_len)
    out = jnp.einsum('ble,ed->bld', jnp.concatenate([y_a, y_b], axis=-1), w_out)
    return out, finals


def pool_mixer(a, pool_w, pool_scale, n_seg, seg_len):
    bsz, n, _ = a.shape
    xs = a.astype(jnp.float32).reshape(bsz, n_seg, seg_len, D_MODEL)
    cs = jnp.pad(jnp.cumsum(xs, axis=2), ((0, 0), (0, 0), (1, 0), (0, 0)))
    pos = jnp.arange(seg_len)
    outs = []
    for gi, w in enumerate(POOL_WINDOWS):
        sl = slice(gi * POOL_GROUP, (gi + 1) * POOL_GROUP)
        lo = jnp.clip(pos - w // 2, 0, seg_len - 1)
        hi = jnp.clip(pos + w // 2 - 1, 0, seg_len - 1)
        cnt = (hi - lo + 1).astype(jnp.float32)[:, None]
        csg = cs[..., sl]
        mean = (jnp.take(csg, hi + 1, axis=2) - jnp.take(csg, lo, axis=2)) / cnt
        z = mean - xs[..., sl]
        outs.append(jnp.einsum('bnwc,cd->bnwd', z, pool_w[gi].astype(jnp.float32)))
    y = jnp.concatenate(outs, axis=-1) * pool_scale.astype(jnp.float32)
    return y.reshape(bsz, n, D_MODEL).astype(a.dtype)


def peer_ffn(h, wq, keys, u_tab, v_tab):
    bsz, n, _ = h.shape
    hb = h.reshape(bsz * n // TOK_BLOCK, TOK_BLOCK, D_MODEL)
    keys32 = keys.astype(jnp.float32)

    def block(xb):
        q = (xb @ wq).astype(jnp.float32).reshape(TOK_BLOCK, PEER_HEADS, 2, PEER_HALF)
        s = jnp.einsum('thsk,snk->thsn', q, keys32)
        top_s, top_i = lax.top_k(s, PEER_TOPK)
        cand_s = (top_s[:, :, 0, :, None] + top_s[:, :, 1, None, :]).reshape(
            TOK_BLOCK, PEER_HEADS, PEER_TOPK * PEER_TOPK)
        cand_i = (top_i[:, :, 0, :, None] * PEER_NKEYS + top_i[:, :, 1, None, :]).reshape(
            TOK_BLOCK, PEER_HEADS, PEER_TOPK * PEER_TOPK)
        best_s, best_j = lax.top_k(cand_s, PEER_TOPK)
        idx = jnp.take_along_axis(cand_i, best_j, axis=-1)
        gate = jax.nn.softmax(best_s, axis=-1)
        u = jnp.take(u_tab, idx, axis=0)
        act = jax.nn.gelu(jnp.einsum('thkd,td->thk', u, xb).astype(jnp.float32), approximate=False)
        coef = (gate * act).astype(xb.dtype)
        v = jnp.take(v_tab, idx, axis=0)
        return jnp.einsum('thk,thkd->td', coef, v)

    return lax.map(block, hb).reshape(bsz, n, D_MODEL)


def setup_inputs(seed: int = 0) -> dict:
    key = jax.random.key(seed)
    ks = jax.random.split(key, 30)
    f32 = jnp.float32

    def nrm(k, shape, s):
        return jax.random.normal(k, shape, f32) * s

    G, P, I = SSM_GROUPS, SSM_STATE, SSM_GROUP
    n_idx = jnp.arange(P, dtype=f32)
    return {
        'x': nrm(ks[0], (BATCH, SEQ, D_MODEL), 1.0),
        'c': nrm(ks[1], (BATCH, D_MODEL), 1.0),
        'ctx': nrm(ks[2], (BATCH, CTX_LEN, D_MODEL), 1.0),
        'c_ctx': nrm(ks[3], (D_MODEL,), 1.0),
        'ada_w': nrm(ks[4], (DEPTH, D_MODEL, 6 * D_MODEL), 0.5 * D_MODEL ** -0.5),
        'ada_b': nrm(ks[5], (DEPTH, 6 * D_MODEL), 0.01),
        'ln_mix_g': 1.0 + nrm(ks[6], (DEPTH, D_MODEL), 0.01),
        'ln_mix_b': nrm(ks[7], (DEPTH, D_MODEL), 0.01),
        'ln_ffn_g': 1.0 + nrm(ks[8], (DEPTH, D_MODEL), 0.01),
        'ln_ffn_b': nrm(ks[9], (DEPTH, D_MODEL), 0.01),
        'even_w_in': nrm(ks[10], (N_EVEN, D_MODEL, EVEN_PROJ), D_MODEL ** -0.5),
        'even_w_out': nrm(ks[11], (N_EVEN, D_MODEL, D_MODEL), DN_BETA * D_MODEL ** -0.5),
        'ssm_lam_re': -0.5 + nrm(ks[12], (N_EVEN, 2, G, P), 0.01),
        'ssm_lam_im': math.pi * n_idx + nrm(ks[13], (N_EVEN, 2, G, P), 0.01),
        'ssm_log_step': jax.random.uniform(ks[14], (N_EVEN, 2, G), f32,
                                           math.log(STEP_MIN), math.log(STEP_MAX)),
        'ssm_b_re': nrm(ks[15], (N_EVEN, 2, G, P, I), (2 * I) ** -0.5),
        'ssm_b_im': nrm(ks[16], (N_EVEN, 2, G, P, I), (2 * I) ** -0.5),
        'ssm_c_re': nrm(ks[17], (N_EVEN, 2, G, I, P), P ** -0.5),
        'ssm_c_im': nrm(ks[18], (N_EVEN, 2, G, I, P), P ** -0.5),
        'ssm_d': nrm(ks[19], (N_EVEN, SSM_WIDTH), 1.0),
        'glu_w': nrm(ks[20], (N_EVEN, SSM_WIDTH, SSM_WIDTH), SSM_WIDTH ** -0.5),
        'glu_b': nrm(ks[21], (N_EVEN, SSM_WIDTH), 0.01),
        'conv_w': nrm(ks[22], (N_EVEN, CONV_K, CONV_WIDTH), CONV_K ** -0.5),
        'conv_b': nrm(ks[23], (N_EVEN, CONV_WIDTH), 0.01),
        'pool_w': nrm(ks[24], (N_ODD, len(POOL_WINDOWS), POOL_GROUP, POOL_GROUP),
                      DN_BETA * POOL_GROUP ** -0.5),
        'pool_scale': 1.0 + nrm(ks[25], (N_ODD, D_MODEL), 0.1),
        'peer_wq': nrm(ks[26], (DEPTH, D_MODEL, PEER_HEADS * PEER_QDIM), D_MODEL ** -0.5),
        'peer_keys': nrm(ks[27], (DEPTH, 2, PEER_NKEYS, PEER_HALF), PEER_HALF ** -0.5),
        'peer_u': nrm(ks[28], (DEPTH, PEER_EXPERTS, D_MODEL), D_MODEL ** -0.5),
        'peer_v': nrm(ks[29], (DEPTH, PEER_EXPERTS, D_MODEL), DN_BETA * PEER_HEADS ** -0.5),
    }


def reference(x, c, ctx, c_ctx, ada_w, ada_b, ln_mix_g, ln_mix_b, ln_ffn_g, ln_ffn_b,
              even_w_in, even_w_out, ssm_lam_re, ssm_lam_im, ssm_log_step, ssm_b_re, ssm_b_im,
              ssm_c_re, ssm_c_im, ssm_d, glu_w, glu_b, conv_w, conv_b, pool_w, pool_scale,
              peer_wq, peer_keys, peer_u, peer_v):
    seq = x.shape[1]
    rows = seq // GRID_W
    h_lat, h_ctx = x, ctx
    for layer in range(DEPTH):
        ctx_out = any(j > layer and j % 2 == 0 for j in range(DEPTH))
        even = layer % 2 == 0
        m_lat = jnp.einsum('bd,de->be', jax.nn.silu(c), ada_w[layer]) + ada_b[layer]
        sh1, sc1, g1, sh2, sc2, g2 = jnp.split(m_lat[:, None, :], 6, axis=-1)
        a_lat = h_lat * (1 + sc1) + sh1
        if ctx_out or even:
            m_ctx = jnp.einsum('d,de->e', jax.nn.silu(c_ctx), ada_w[layer]) + ada_b[layer]
            csh1, csc1, cg1, csh2, csc2, cg2 = jnp.split(m_ctx, 6)
            a_ctx = h_ctx * (1 + csc1) + csh1
        if even:
            e = layer // 2
            ssm = (ssm_lam_re[e], ssm_lam_im[e], ssm_log_step[e], ssm_b_re[e], ssm_b_im[e],
                   ssm_c_re[e], ssm_c_im[e], ssm_d[e])
            if ctx_out:
                y_ctx, finals = even_mixer(a_ctx, even_w_in[e], even_w_out[e], ssm, glu_w[e], glu_b[e],
                                           conv_w[e], conv_b[e], 1, CTX_LEN, None)
            else:
                u_ctx = jnp.einsum('bld,de->ble', a_ctx, even_w_in[e][:, :SSM_WIDTH])
                _, finals = s5_bidir(u_ctx, *ssm, h0s=None, with_output=False)
            y_lat, _ = even_mixer(a_lat, even_w_in[e], even_w_out[e], ssm, glu_w[e], glu_b[e],
                                  conv_w[e], conv_b[e], rows, GRID_W, finals)
        else:
            o = layer // 2
            y_lat = pool_mixer(a_lat, pool_w[o], pool_scale[o], rows, GRID_W)
            if ctx_out:
                y_ctx = pool_mixer(a_ctx, pool_w[o], pool_scale[o], 1, CTX_LEN)
        h_lat = layer_norm(DN_ALPHA * h_lat + g1 * y_lat, ln_mix_g[layer], ln_mix_b[layer])
        f_lat = peer_ffn(h_lat * (1 + sc2) + sh2, peer_wq[layer], peer_keys[layer],
                         peer_u[layer], peer_v[layer])
        h_lat = layer_norm(DN_ALPHA * h_lat + g2 * f_lat, ln_ffn_g[layer], ln_ffn_b[layer])
        if ctx_out:
            h_ctx = layer_norm(DN_ALPHA * h_ctx + cg1 * y_ctx, ln_mix_g[layer], ln_mix_b[layer])
            f_ctx = peer_ffn(h_ctx * (1 + csc2) + csh2, peer_wq[layer], peer_keys[layer],
                             peer_u[layer], peer_v[layer])
            h_ctx = layer_norm(DN_ALPHA * h_ctx + cg2 * f_ctx, ln_ffn_g[layer], ln_ffn_b[layer])
    return h_lat
```

```python
import functools
import math

import numpy as np
import jax
import jax.numpy as jnp
from jax import lax
from jax.experimental import pallas as pl
from jax.experimental.pallas import tpu as pltpu

F32 = jnp.float32
HI = lax.Precision.HIGHEST

GRID_W = 64
POOL_WINDOWS = (2, 4, 8, 16)
PEER_TOPK = 16
LN_EPS = 1e-5
STEP_CHUNK = 64
VMEM_LIMIT_V7X = 56 * 1024 * 1024
NT_DIMS = (((1,), (1,)), ((), ()))


def _cparams(*sem):
    return pltpu.CompilerParams(dimension_semantics=sem, vmem_limit_bytes=VMEM_LIMIT_V7X)


def _dot(a, b):
    return jnp.dot(a, b, preferred_element_type=F32, precision=HI)


def _dot_nt(a, b):
    return lax.dot_general(a, b, NT_DIMS, preferred_element_type=F32, precision=HI)


def _gelu(x):
    return 0.5 * x * (1.0 + lax.erf(x * (1.0 / math.sqrt(2.0))))


def _layer_norm(x, g, b):
    mu = jnp.mean(x, axis=-1, keepdims=True)
    xc = x - mu
    var = jnp.mean(xc * xc, axis=-1, keepdims=True)
    return xc * lax.rsqrt(var + LN_EPS) * g + b


def _token_tile(n, cap):
    t = min(n, cap)
    assert n % t == 0
    return t


def _ada_kernel(c_ref, w_ref, b_ref, o_ref):
    c = c_ref[...]
    s = c * jax.nn.sigmoid(c)
    o_ref[0] = _dot(s, w_ref[0]) + b_ref[0]


def _ada_modulation(c_all, ada_w, ada_b):
    depth, d, e = ada_w.shape
    rows = c_all.shape[0]
    te = e // 6
    return pl.pallas_call(
        _ada_kernel,
        grid=(depth, e // te),
        in_specs=[pl.BlockSpec((rows, d), lambda l, j: (0, 0)),
                  pl.BlockSpec((1, d, te), lambda l, j: (l, 0, j)),
                  pl.BlockSpec((1, 1, te), lambda l, j: (l, 0, j))],
        out_specs=pl.BlockSpec((1, rows, te), lambda l, j: (l, 0, j)),
        out_shape=jax.ShapeDtypeStruct((depth, rows, e), F32),
        compiler_params=_cparams("arbitrary", "arbitrary"),
        name="ada_modulation",
    )(c_all, ada_w, ada_b.reshape(depth, 1, e))


def _inproj_kernel(ssm_w, h_ref, sc_ref, sh_ref, w_ref, ps_ref, pc_ref):
    a = h_ref[0] * (1.0 + sc_ref[0]) + sh_ref[0]
    ps_ref[0] = _dot(a, w_ref[:, :ssm_w])
    pc_ref[0] = _dot(a, w_ref[:, ssm_w:])


def _in_projection(h, sc, sh, w_in, ssm_w):
    bsz, n, d = h.shape
    e = w_in.shape[1]
    tl = _token_tile(n, 512)
    tok = lambda b, j: (b, j, 0)
    mod = lambda b, j: (b, 0, 0)
    return pl.pallas_call(
        functools.partial(_inproj_kernel, ssm_w),
        grid=(bsz, n // tl),
        in_specs=[pl.BlockSpec((1, tl, d), tok),
                  pl.BlockSpec((1, 1, d), mod),
                  pl.BlockSpec((1, 1, d), mod),
                  pl.BlockSpec((d, e), lambda b, j: (0, 0))],
        out_specs=[pl.BlockSpec((1, tl, ssm_w), tok),
                   pl.BlockSpec((1, tl, e - ssm_w), tok)],
        out_shape=[jax.ShapeDtypeStruct((bsz, n, ssm_w), F32),
                   jax.ShapeDtypeStruct((bsz, n, e - ssm_w), F32)],
        compiler_params=_cparams("arbitrary", "arbitrary"),
        name="even_in_projection",
    )(h, sc, sh, w_in)


def _s5_param_kernel(chunk, lam_row_ref, lam_col_ref, ls_ref, bt_re_ref, bt_im_ref,
                     ct_re_ref, ct_im_ref, k_ref, w_re_ref, w_im_ref, g_re_ref, g_im_ref, a_ref):
    backward = pl.program_id(0) == 1
    p_dim = lam_row_ref.shape[-1]
    i_dim = k_ref.shape[-1]
    rows = chunk * i_dim
    step = jnp.exp(ls_ref[0, 0])
    lam_row = lam_row_ref[0, 0]
    lr_row, li_row = lam_row[0:1, :], lam_row[1:2, :]
    lam_col = lam_col_ref[0, 0]
    lr_col, li_col = lam_col[:, 0:1], lam_col[:, 1:2]
    ldr_row, ldi_row = lr_row * step, li_row * step
    ldr_col, ldi_col = lr_col * step, li_col * step

    def power_rows(expo):
        mag = jnp.exp(expo * ldr_row)
        return mag * jnp.cos(expo * ldi_row), mag * jnp.sin(expo * ldi_row)

    mag = jnp.exp(ldr_col)
    ab_re, ab_im = mag * jnp.cos(ldi_col), mag * jnp.sin(ldi_col)
    den = lr_col * lr_col + li_col * li_col
    nr = ab_re - 1.0
    f_re = (nr * lr_col + ab_im * li_col) / den
    f_im = (ab_im * lr_col - nr * li_col) / den
    bt_re, bt_im = bt_re_ref[0, 0], bt_im_ref[0, 0]
    bbt_re = f_re * bt_re - f_im * bt_im
    bbt_im = f_re * bt_im + f_im * bt_re
    bb_re, bb_im = bbt_re[:, :i_dim], bbt_im[:, :i_dim]

    ct_re, ct_im = ct_re_ref[0, 0], ct_im_ref[0, 0]
    tok = (lax.broadcasted_iota(jnp.int32, (rows, 1), 0) // i_dim).astype(F32)

    e_re, e_im = power_rows(tok)
    ck_re = ct_re * e_re - ct_im * e_im
    ck_im = ct_re * e_im + ct_im * e_re
    k_ref[0, 0] = _dot(ck_re, bb_re) - _dot(ck_im, bb_im)

    f_exp = jnp.where(backward, float(chunk) - tok, tok + 1.0)
    e_re, e_im = power_rows(f_exp)
    g_re_ref[0, 0] = ct_re * e_re - ct_im * e_im
    g_im_ref[0, 0] = ct_re * e_im + ct_im * e_re

    tok_l = (lax.broadcasted_iota(jnp.int32, (1, rows), 1) // i_dim).astype(F32)
    e_exp = jnp.where(backward, tok_l, float(chunk - 1) - tok_l)
    magw = jnp.exp(ldr_col * e_exp)
    ew_re, ew_im = magw * jnp.cos(ldi_col * e_exp), magw * jnp.sin(ldi_col * e_exp)
    w_re_ref[0, 0] = ew_re * bbt_re - ew_im * bbt_im
    w_im_ref[0, 0] = ew_re * bbt_im + ew_im * bbt_re

    full = jnp.full((1, 1), float(chunk), F32)
    a_re, a_im = power_rows(full)
    a_ref[0, 0] = jnp.concatenate([a_re, a_im], axis=0)


def _s5_operators(lam_re, lam_im, log_step, b_re, b_im, c_re, c_im):
    _, g, p = lam_re.shape
    i = b_re.shape[-1]
    t = STEP_CHUNK
    rows = t * i
    lam_row = jnp.stack([lam_re, lam_im], axis=2)
    lam_col = jnp.stack([lam_re, lam_im], axis=3)
    ls = log_step.reshape(2, g, 1, 1)
    bt_re = jnp.tile(b_re, (1, 1, 1, t))
    bt_im = jnp.tile(b_im, (1, 1, 1, t))
    ct_re = jnp.tile(c_re, (1, 1, t, 1))
    ct_im = jnp.tile(c_im, (1, 1, t, 1))
    blk = lambda *shape: pl.BlockSpec((1, 1) + shape, lambda dr, gi: (dr, gi, 0, 0))
    k, w_re, w_im, g_re, g_im, a = pl.pallas_call(
        functools.partial(_s5_param_kernel, t),
        grid=(2, g),
        in_specs=[blk(2, p), blk(p, 2), blk(1, 1), blk(p, rows), blk(p, rows),
                  blk(rows, p), blk(rows, p)],
        out_specs=[blk(rows, i), blk(p, rows), blk(p, rows), blk(rows, p), blk(rows, p), blk(2, p)],
        out_shape=[jax.ShapeDtypeStruct((2, g, rows, i), F32),
                   jax.ShapeDtypeStruct((2, g, p, rows), F32),
                   jax.ShapeDtypeStruct((2, g, p, rows), F32),
                   jax.ShapeDtypeStruct((2, g, rows, p), F32),
                   jax.ShapeDtypeStruct((2, g, rows, p), F32),
                   jax.ShapeDtypeStruct((2, g, 2, p), F32)],
        compiler_params=_cparams("arbitrary", "arbitrary"),
        name="s5_operators",
    )(lam_row, lam_col, ls, bt_re, bt_im, ct_re, ct_im)

    k5 = k.reshape(2, g, t, i, i)
    s_idx = np.arange(t)[:, None]
    t_idx = np.arange(t)[None, :]

    def toeplitz(kd, lag):
        m = kd[:, np.clip(lag, 0, t - 1)]
        m = jnp.where((lag >= 0)[None, :, :, None, None], m, 0.0)
        return m.transpose(0, 1, 4, 2, 3).reshape(g, rows, rows)

    m = jnp.stack([toeplitz(k5[0], t_idx - s_idx), toeplitz(k5[1], s_idx - t_idx)])
    return m, w_re, w_im, g_re, g_im, a


def _s5_scan_kernel(bsz, u_ref, d_ref, m_ref, w_re_ref, w_im_ref, g_re_ref, g_im_ref, a_ref,
                    h0_re_ref, h0_im_ref, y_ref, hf_re_ref, hf_im_ref,
                    z_re_scr, z_im_scr, hp_re_scr, hp_im_scr):
    backward = pl.program_id(1) == 1
    u = u_ref[0]
    n_chunks = u.shape[0] // bsz
    z_re_scr[...] = _dot_nt(u, w_re_ref[0, 0])
    z_im_scr[...] = _dot_nt(u, w_im_ref[0, 0])
    a = a_ref[0, 0]
    a_re, a_im = a[0:1, :], a[1:2, :]

    def chunk_step(k, carry):
        h_re, h_im = carry
        c = jnp.where(backward, n_chunks - 1 - k, k)
        rows = pl.ds(pl.multiple_of(c * bsz, bsz), bsz)
        hp_re_scr[rows, :] = h_re
        hp_im_scr[rows, :] = h_im
        n_re = a_re * h_re - a_im * h_im + z_re_scr[rows, :]
        n_im = a_re * h_im + a_im * h_re + z_im_scr[rows, :]
        return n_re, n_im

    h_re, h_im = lax.fori_loop(0, n_chunks, chunk_step, (h0_re_ref[0, 0], h0_im_ref[0, 0]))
    hf_re_ref[0, 0] = h_re
    hf_im_ref[0, 0] = h_im

    y = (_dot(u, m_ref[0, 0]) + _dot_nt(hp_re_scr[...], g_re_ref[0, 0])
         - _dot_nt(hp_im_scr[...], g_im_ref[0, 0]))

    @pl.when(jnp.logical_not(backward))
    def _():
        y_ref[0] = d_ref[0] * u + y

    @pl.when(backward)
    def _():
        y_ref[0] = y_ref[0] + y


def _s5_bidir(u, ops, d_skip, h0):
    m, w_re, w_im, g_re, g_im, a = ops
    bsz, n, width = u.shape
    _, g, p, rows = w_re.shape
    t = STEP_CHUNK
    i = rows // t
    nc = n // t
    r = nc * bsz
    ug = u.reshape(bsz, nc, t, g, i).transpose(3, 1, 0, 2, 4).reshape(g, r, rows)
    dg = jnp.tile(d_skip.reshape(g, 1, i), (1, 1, t))
    if h0 is None:
        h0 = (jnp.zeros((2, g, bsz, p), F32),) * 2
    per_g = lambda *shape: pl.BlockSpec((1,) + shape, lambda gi, dr: (gi, 0, 0))
    per_dg = lambda *shape: pl.BlockSpec((1, 1) + shape, lambda gi, dr: (dr, gi, 0, 0))
    y, hf_re, hf_im = pl.pallas_call(
        functools.partial(_s5_scan_kernel, bsz),
        grid=(g, 2),
        in_specs=[per_g(r, rows), per_g(1, rows), per_dg(rows, rows), per_dg(p, rows), per_dg(p, rows),
                  per_dg(rows, p), per_dg(rows, p), per_dg(2, p), per_dg(bsz, p), per_dg(bsz, p)],
        out_specs=[per_g(r, rows), per_dg(bsz, p), per_dg(bsz, p)],
        out_shape=[jax.ShapeDtypeStruct((g, r, rows), F32),
                   jax.ShapeDtypeStruct((2, g, bsz, p), F32),
                   jax.ShapeDtypeStruct((2, g, bsz, p), F32)],
        scratch_shapes=[pltpu.VMEM((r, p), F32)] * 4,
        compiler_params=_cparams("arbitrary", "arbitrary"),
        name="s5_scan",
    )(ug, dg, m, w_re, w_im, g_re, g_im, a, h0[0], h0[1])
    y = y.reshape(g, nc, bsz, t, i).transpose(2, 1, 3, 0, 4).reshape(bsz, n, width)
    return y, (hf_re, hf_im)


def _even_post_kernel(seg_len, alpha, ys_ref, pc_ref, h_ref, g1_ref, wout_ref, gluw_ref, glub_ref,
                      cw_ref, cb_ref, lng_ref, lnb_ref, o_ref):
    ys = ys_ref[0]
    tl, sw = ys.shape
    g = _gelu(ys)
    ya = g * jax.nn.sigmoid(_dot(g, gluw_ref[...]) + glub_ref[...])
    cwid = pc_ref.shape[-1] // 3
    z = pc_ref[0, :, cwid:2 * cwid] * pc_ref[0, :, 2 * cwid:]
    pos = lax.broadcasted_iota(jnp.int32, (tl, 1), 0) % seg_len
    z_prev = jnp.where(pos == 0, 0.0, pltpu.roll(z, 1, axis=0))
    z_next = jnp.where(pos == seg_len - 1, 0.0, pltpu.roll(z, tl - 1, axis=0))
    cw = cw_ref[...]
    conv = cw[0:1, :] * z_prev + cw[1:2, :] * z + cw[2:3, :] * z_next + cb_ref[...]
    yb = pc_ref[0, :, :cwid] * conv
    out = _dot(ya, wout_ref[:sw, :]) + _dot(yb, wout_ref[sw:, :])
    h = h_ref[0]
    o_ref[0] = _layer_norm(alpha * h + g1_ref[0] * out, lng_ref[...], lnb_ref[...])


def _even_post(y_ssm, p_conv, h, g1, w_out, glu_w, glu_b, conv_w, conv_b, ln_g, ln_b, seg_len, alpha):
    bsz, n, d = h.shape
    sw = y_ssm.shape[-1]
    cw3 = p_conv.shape[-1]
    cwid = cw3 // 3
    tl = _token_tile(n, 512)
    assert tl % seg_len == 0
    tok = lambda b, j: (b, j, 0)
    mod = lambda b, j: (b, 0, 0)
    full = lambda *shape: pl.BlockSpec(shape, lambda b, j: (0,) * len(shape))
    return pl.pallas_call(
        functools.partial(_even_post_kernel, seg_len, alpha),
        grid=(bsz, n // tl),
        in_specs=[pl.BlockSpec((1, tl, sw), tok), pl.BlockSpec((1, tl, cw3), tok),
                  pl.BlockSpec((1, tl, d), tok), pl.BlockSpec((1, 1, d), mod),
                  full(sw + cwid, d), full(sw, sw), full(1, sw), full(3, cwid), full(1, cwid),
                  full(1, d), full(1, d)],
        out_specs=pl.BlockSpec((1, tl, d), tok),
        out_shape=jax.ShapeDtypeStruct((bsz, n, d), F32),
        compiler_params=_cparams("arbitrary", "arbitrary"),
        name="even_post",
    )(y_ssm, p_conv, h, g1, w_out, glu_w, glu_b.reshape(1, sw), conv_w, conv_b.reshape(1, cwid),
      ln_g.reshape(1, d), ln_b.reshape(1, d))


def _pool_matrices(seg_len, tile):
    pos = np.arange(tile) % seg_len
    seg = np.arange(tile) // seg_len
    mats = []
    for w in POOL_WINDOWS:
        lo = np.clip(pos - w // 2, 0, seg_len - 1)
        hi = np.clip(pos + w // 2 - 1, 0, seg_len - 1)
        cnt = (hi - lo + 1).astype(np.float64)
        inside = ((seg[:, None] == seg[None, :]) & (pos[None, :] >= lo[:, None])
                  & (pos[None, :] <= hi[:, None]))
        mats.append(inside / cnt[:, None] - np.eye(tile))
    return np.stack(mats).astype(np.float32)


def _pool_kernel(alpha, h_ref, sc_ref, sh_ref, g1_ref, pm_ref, pw_ref, ps_ref, lng_ref, lnb_ref, o_ref):
    h = h_ref[0]
    a = h * (1.0 + sc_ref[0]) + sh_ref[0]
    n_win, pg, _ = pw_ref.shape
    ys = []
    for gi in range(n_win):
        z = _dot(pm_ref[gi], a[:, gi * pg:(gi + 1) * pg])
        ys.append(_dot(z, pw_ref[gi]))
    y = jnp.concatenate(ys, axis=-1) * ps_ref[...]
    o_ref[0] = _layer_norm(alpha * h + g1_ref[0] * y, lng_ref[...], lnb_ref[...])


def _pool_mixer(h, sc, sh, g1, pool_w, pool_scale, ln_g, ln_b, seg_len, alpha):
    bsz, n, d = h.shape
    tl = _token_tile(n, 256)
    assert tl % seg_len == 0
    pm = jnp.asarray(_pool_matrices(seg_len, tl))
    n_win, pg, _ = pool_w.shape
    tok = lambda b, j: (b, j, 0)
    mod = lambda b, j: (b, 0, 0)
    full = lambda *shape: pl.BlockSpec(shape, lambda b, j: (0,) * len(shape))
    return pl.pallas_call(
        functools.partial(_pool_kernel, alpha),
        grid=(bsz, n // tl),
        in_specs=[pl.BlockSpec((1, tl, d), tok), pl.BlockSpec((1, 1, d), mod), pl.BlockSpec((1, 1, d), mod),
                  pl.BlockSpec((1, 1, d), mod), full(n_win, tl, tl), full(n_win, pg, pg), full(1, d),
                  full(1, d), full(1, d)],
        out_specs=pl.BlockSpec((1, tl, d), tok),
        out_shape=jax.ShapeDtypeStruct((bsz, n, d), F32),
        compiler_params=_cparams("arbitrary", "arbitrary"),
        name="pool_mixer",
    )(h, sc, sh, g1, pm, pool_w, pool_scale.reshape(1, d), ln_g.reshape(1, d), ln_b.reshape(1, d))


def _top_rows(s, n_top):
    n = s.shape[0]
    rid = lax.broadcasted_iota(jnp.int32, s.shape, 0)
    vals, rows = [], []
    for _ in range(n_top):
        m = jnp.max(s, axis=0, keepdims=True)
        sel = jnp.min(jnp.where(s == m, rid, n), axis=0, keepdims=True)
        vals.append(m)
        rows.append(sel)
        s = jnp.where(rid == sel, -jnp.inf, s)
    return vals, rows


def _peer_route_kernel(n_heads, h_ref, sc_ref, sh_ref, wq_ref, keys_ref, idx_ref, gate_ref):
    xin = h_ref[0] * (1.0 + sc_ref[0]) + sh_ref[0]
    n_keys, half = keys_ref.shape[1], keys_ref.shape[2]
    k_top = PEER_TOPK
    for head in range(n_heads):
        tops = []
        for side in range(2):
            col = (head * 2 + side) * half
            q = _dot(xin, wq_ref[:, col:col + half])
            s = _dot_nt(keys_ref[side], q)
            vals, rows = _top_rows(s, k_top)
            tops.append((vals, jnp.concatenate(rows, axis=0)))
        (v0, i0), (v1, i1) = tops
        v1_all = jnp.concatenate(v1, axis=0)
        cand_s = jnp.concatenate([v0[a] + v1_all for a in range(k_top)], axis=0)
        cand_i = jnp.concatenate([i0[a:a + 1, :] * n_keys + i1 for a in range(k_top)], axis=0)
        rid = lax.broadcasted_iota(jnp.int32, cand_s.shape, 0)
        best_s, best_i = [], []
        s = cand_s
        for _ in range(k_top):
            m = jnp.max(s, axis=0, keepdims=True)
            sel = jnp.min(jnp.where(s == m, rid, k_top * k_top), axis=0, keepdims=True)
            hit = rid == sel
            best_s.append(m)
            best_i.append(jnp.sum(jnp.where(hit, cand_i, 0), axis=0, keepdims=True))
            s = jnp.where(hit, -jnp.inf, s)
        bs = jnp.concatenate(best_s, axis=0)
        e = jnp.exp(bs - bs[0:1, :])
        gate = e / jnp.sum(e, axis=0, keepdims=True)
        idx_ref[0, head * k_top:(head + 1) * k_top, :] = jnp.concatenate(best_i, axis=0)
        gate_ref[0, head * k_top:(head + 1) * k_top, :] = gate


def _peer_route(h, sc, sh, wq, keys):
    bsz, n, d = h.shape
    _, n_keys, half = keys.shape
    n_heads = wq.shape[1] // (2 * half)
    tt = _token_tile(n, 256)
    hk = n_heads * PEER_TOPK
    mod = lambda b, j: (b, 0, 0)
    return pl.pallas_call(
        functools.partial(_peer_route_kernel, n_heads),
        grid=(bsz, n // tt),
        in_specs=[pl.BlockSpec((1, tt, d), lambda b, j: (b, j, 0)),
                  pl.BlockSpec((1, 1, d), mod), pl.BlockSpec((1, 1, d), mod),
                  pl.BlockSpec(wq.shape, lambda b, j: (0, 0)),
                  pl.BlockSpec(keys.shape, lambda b, j: (0, 0, 0))],
        out_specs=[pl.BlockSpec((1, hk, tt), lambda b, j: (b, 0, j)),
                   pl.BlockSpec((1, hk, tt), lambda b, j: (b, 0, j))],
        out_shape=[jax.ShapeDtypeStruct((bsz, hk, n), jnp.int32),
                   jax.ShapeDtypeStruct((bsz, hk, n), F32)],
        compiler_params=_cparams("arbitrary", "arbitrary"),
        name="peer_route",
    )(h, sc, sh, wq, keys)


def _peer_expert_kernel(alpha, idx_ref, gate_ref, h_ref, sc_ref, sh_ref, g2_ref, lng_ref, lnb_ref,
                        u_hbm, v_hbm, o_ref, ubuf, vbuf, xin_scr, f_scr, sems):
    tb, n_sel = gate_ref.shape[1], gate_ref.shape[2]
    h = h_ref[0]
    xin_scr[...] = h * (1.0 + sc_ref[0]) + sh_ref[0]

    def row_copy(tab, buf, e, slot, k, sem):
        return pltpu.make_async_copy(tab.at[pl.ds(e, 1)], buf.at[slot, pl.ds(k, 1)], sem)

    def issue(t, slot):
        def body(k, carry):
            e = idx_ref[t, k]
            row_copy(u_hbm, ubuf, e, slot, k, sems.at[0, slot]).start()
            row_copy(v_hbm, vbuf, e, slot, k, sems.at[1, slot]).start()
            return carry
        lax.fori_loop(0, n_sel, body, 0, unroll=8)

    def wait(slot):
        pltpu.make_async_copy(u_hbm.at[pl.ds(0, n_sel)], ubuf.at[slot], sems.at[0, slot]).wait()
        pltpu.make_async_copy(v_hbm.at[pl.ds(0, n_sel)], vbuf.at[slot], sems.at[1, slot]).wait()

    eye = (lax.broadcasted_iota(jnp.int32, (n_sel, n_sel), 0)
           == lax.broadcasted_iota(jnp.int32, (n_sel, n_sel), 1))

    issue(0, 0)

    def token(t, carry):
        slot = t % 2

        @pl.when(t + 1 < tb)
        def _():
            issue(t + 1, 1 - slot)

        wait(slot)
        x = xin_scr[pl.ds(t, 1), :]
        act = _gelu(jnp.sum(ubuf[slot] * x, axis=1, keepdims=True))
        g_row = gate_ref[0, pl.ds(t, 1), :]
        g_col = jnp.sum(jnp.where(eye, g_row, 0.0), axis=1, keepdims=True)
        f_scr[pl.ds(t, 1), :] = jnp.sum((g_col * act) * vbuf[slot], axis=0, keepdims=True)
        return carry

    lax.fori_loop(0, tb, token, 0)
    o_ref[0] = _layer_norm(alpha * h + g2_ref[0] * f_scr[...], lng_ref[...], lnb_ref[...])


def _peer_experts(idx, gate, h, sc, sh, g2, ln_g, ln_b, u_tab, v_tab, alpha):
    bsz, n, d = h.shape
    n_sel = idx.shape[-1]
    tb = _token_tile(n, 32)
    nb = n // tb
    tok = lambda b, j: (b, j, 0)
    mod = lambda b, j: (b, 0, 0)
    full = lambda *shape: pl.BlockSpec(shape, lambda b, j: (0,) * len(shape))
    return pl.pallas_call(
        functools.partial(_peer_expert_kernel, alpha),
        grid=(bsz, nb),
        in_specs=[pl.BlockSpec((tb, n_sel), lambda b, j: (b * nb + j, 0), memory_space=pltpu.SMEM),
                  pl.BlockSpec((1, tb, n_sel), tok), pl.BlockSpec((1, tb, d), tok),
                  pl.BlockSpec((1, 1, d), mod), pl.BlockSpec((1, 1, d), mod), pl.BlockSpec((1, 1, d), mod),
                  full(1, d), full(1, d),
                  pl.BlockSpec(memory_space=pl.ANY), pl.BlockSpec(memory_space=pl.ANY)],
        out_specs=pl.BlockSpec((1, tb, d), tok),
        out_shape=jax.ShapeDtypeStruct((bsz, n, d), F32),
        scratch_shapes=[pltpu.VMEM((2, n_sel, d), F32), pltpu.VMEM((2, n_sel, d), F32),
                        pltpu.VMEM((tb, d), F32), pltpu.VMEM((tb, d), F32),
                        pltpu.SemaphoreType.DMA((2, 2))],
        compiler_params=_cparams("arbitrary", "arbitrary"),
        name="peer_experts",
    )(idx.reshape(bsz * n, n_sel), gate, h, sc, sh, g2, ln_g.reshape(1, d), ln_b.reshape(1, d),
      u_tab, v_tab)


def _peer_block(h, sc2, sh2, g2, wq, keys, u_tab, v_tab, ln_g, ln_b, alpha):
    idx, gate = _peer_route(h, sc2, sh2, wq, keys)
    idx = idx.transpose(0, 2, 1)
    gate = gate.transpose(0, 2, 1)
    return _peer_experts(idx, gate, h, sc2, sh2, g2, ln_g, ln_b, u_tab, v_tab, alpha)


def kernel(x, c, ctx, c_ctx, ada_w, ada_b, ln_mix_g, ln_mix_b, ln_ffn_g, ln_ffn_b, even_w_in, even_w_out, ssm_lam_re, ssm_lam_im, ssm_log_step, ssm_b_re, ssm_b_im, ssm_c_re, ssm_c_im, ssm_d, glu_w, glu_b, conv_w, conv_b, pool_w, pool_scale, peer_wq, peer_keys, peer_u, peer_v):
    depth, d, _ = ada_w.shape
    bsz, seq, _ = x.shape
    ctx_len = ctx.shape[1]
    ssm_w = ssm_d.shape[-1]
    alpha = (2.0 * depth) ** 0.25

    pad = (-(bsz + 1)) % 8
    c_all = jnp.concatenate([c, c_ctx[None, :], jnp.zeros((pad, d), F32)], axis=0)
    mod = _ada_modulation(c_all, ada_w, ada_b)

    def six(rows):
        return [rows[:, None, k * d:(k + 1) * d] for k in range(6)]

    h_lat, h_ctx = x, ctx
    for layer in range(depth):
        ctx_out = any(j > layer and j % 2 == 0 for j in range(depth))
        even = layer % 2 == 0
        sh1, sc1, g1, sh2, sc2, g2 = six(mod[layer, :bsz])
        need_ctx = ctx_out or even
        if need_ctx:
            csh1, csc1, cg1, csh2, csc2, cg2 = six(jnp.broadcast_to(mod[layer, bsz:bsz + 1], (bsz, 6 * d)))
        lmg, lmb = ln_mix_g[layer], ln_mix_b[layer]
        if even:
            e = layer // 2
            ops = _s5_operators(ssm_lam_re[e], ssm_lam_im[e], ssm_log_step[e], ssm_b_re[e], ssm_b_im[e],
                                ssm_c_re[e], ssm_c_im[e])
            post = functools.partial(_even_post, w_out=even_w_out[e], glu_w=glu_w[e], glu_b=glu_b[e],
                                     conv_w=conv_w[e], conv_b=conv_b[e], ln_g=lmg, ln_b=lmb, alpha=alpha)
            ps_ctx, pc_ctx = _in_projection(h_ctx, csc1, csh1, even_w_in[e], ssm_w)
            y_ctx, finals = _s5_bidir(ps_ctx, ops, ssm_d[e], None)
            ps_lat, pc_lat = _in_projection(h_lat, sc1, sh1, even_w_in[e], ssm_w)
            y_lat, _ = _s5_bidir(ps_lat, ops, ssm_d[e], finals)
            h_lat = post(y_lat, pc_lat, h_lat, g1, seg_len=GRID_W)
            if ctx_out:
                h_ctx = post(y_ctx, pc_ctx, h_ctx, cg1, seg_len=ctx_len)
        else:
            o = layer // 2
            h_lat = _pool_mixer(h_lat, sc1, sh1, g1, pool_w[o], pool_scale[o], lmg, lmb, GRID_W, alpha)
            if ctx_out:
                h_ctx = _pool_mixer(h_ctx, csc1, csh1, cg1, pool_w[o], pool_scale[o], lmg, lmb, ctx_len, alpha)
        peer = functools.partial(_peer_block, wq=peer_wq[layer], keys=peer_keys[layer], u_tab=peer_u[layer],
                                 v_tab=peer_v[layer], ln_g=ln_ffn_g[layer], ln_b=ln_ffn_b[layer], alpha=alpha)
        h_lat = peer(h_lat, sc2, sh2, g2)
        if ctx_out:
            h_ctx = peer(h_ctx, csc2, csh2, cg2)
    return h_lat
```

```python
import functools
import math

import numpy as np
import jax
import jax.numpy as jnp
from jax import lax
from jax.experimental import pallas as pl
from jax.experimental.pallas import tpu as pltpu

F32 = jnp.float32
HI = lax.Precision.HIGHEST

GRID_W = 64
POOL_WINDOWS = (2, 4, 8, 16)
PEER_TOPK = 16
LN_EPS = 1e-5
STEP_CHUNK = 64
VMEM_LIMIT_V7X = 56 * 1024 * 1024
V7X_SUBLANES, V7X_LANES = 8, 128
NT_DIMS = (((1,), (1,)), ((), ()))


def _cparams(*sem):
    return pltpu.CompilerParams(dimension_semantics=sem, vmem_limit_bytes=VMEM_LIMIT_V7X)


def _dot(a, b):
    return jnp.dot(a, b, preferred_element_type=F32, precision=HI)


def _dot_nt(a, b):
    return lax.dot_general(a, b, NT_DIMS, preferred_element_type=F32, precision=HI)


def _gelu(x):
    return 0.5 * x * (1.0 + lax.erf(x * (1.0 / math.sqrt(2.0))))


def _layer_norm(x, g, b):
    mu = jnp.mean(x, axis=-1, keepdims=True)
    xc = x - mu
    var = jnp.mean(xc * xc, axis=-1, keepdims=True)
    return xc * lax.rsqrt(var + LN_EPS) * g + b


def _token_tile(n, cap):
    t = min(n, cap)
    assert n % t == 0
    return t


def _ada_kernel(c_ref, w_ref, b_ref, o_ref):
    c = c_ref[...]
    s = c * jax.nn.sigmoid(c)
    o_ref[0] = _dot(s, w_ref[0]) + b_ref[0]


def _ada_modulation(c_all, ada_w, ada_b):
    depth, d, e = ada_w.shape
    rows = c_all.shape[0]
    te = e // 6
    return pl.pallas_call(
        _ada_kernel,
        grid=(depth, e // te),
        in_specs=[pl.BlockSpec((rows, d), lambda l, j: (0, 0)),
                  pl.BlockSpec((1, d, te), lambda l, j: (l, 0, j)),
                  pl.BlockSpec((1, 1, te), lambda l, j: (l, 0, j))],
        out_specs=pl.BlockSpec((1, rows, te), lambda l, j: (l, 0, j)),
        out_shape=jax.ShapeDtypeStruct((depth, rows, e), F32),
        compiler_params=_cparams("arbitrary", "arbitrary"),
        name="ada_modulation",
    )(c_all, ada_w, ada_b.reshape(depth, 1, e))


def _inproj_kernel(ssm_w, h_ref, sc_ref, sh_ref, w_ref, ps_ref, pc_ref):
    a = h_ref[0] * (1.0 + sc_ref[0]) + sh_ref[0]
    ps_ref[0] = _dot(a, w_ref[:, :ssm_w])
    pc_ref[0] = _dot(a, w_ref[:, ssm_w:])


def _in_projection(h, sc, sh, w_in, ssm_w):
    bsz, n, d = h.shape
    e = w_in.shape[1]
    tl = _token_tile(n, 512)
    tok = lambda b, j: (b, j, 0)
    mod = lambda b, j: (b, 0, 0)
    return pl.pallas_call(
        functools.partial(_inproj_kernel, ssm_w),
        grid=(bsz, n // tl),
        in_specs=[pl.BlockSpec((1, tl, d), tok),
                  pl.BlockSpec((1, 1, d), mod),
                  pl.BlockSpec((1, 1, d), mod),
                  pl.BlockSpec((d, e), lambda b, j: (0, 0))],
        out_specs=[pl.BlockSpec((1, tl, ssm_w), tok),
                   pl.BlockSpec((1, tl, e - ssm_w), tok)],
        out_shape=[jax.ShapeDtypeStruct((bsz, n, ssm_w), F32),
                   jax.ShapeDtypeStruct((bsz, n, e - ssm_w), F32)],
        compiler_params=_cparams("arbitrary", "arbitrary"),
        name="even_in_projection",
    )(h, sc, sh, w_in)


def _s5_param_kernel(chunk, lam_row_ref, lam_col_ref, ls_ref, bt_re_ref, bt_im_ref,
                     ct_re_ref, ct_im_ref, k_ref, w_re_ref, w_im_ref, g_re_ref, g_im_ref, a_ref):
    backward = pl.program_id(0) == 1
    p_dim = lam_row_ref.shape[-1]
    i_dim = k_ref.shape[-1]
    rows = chunk * i_dim
    step = jnp.exp(ls_ref[0, 0])
    lam_row = lam_row_ref[0, 0]
    lr_row, li_row = lam_row[0:1, :], lam_row[1:2, :]
    lam_col = lam_col_ref[0, 0]
    lr_col, li_col = lam_col[:, 0:1], lam_col[:, 1:2]
    ldr_row, ldi_row = lr_row * step, li_row * step
    ldr_col, ldi_col = lr_col * step, li_col * step

    def power_rows(expo):
        mag = jnp.exp(expo * ldr_row)
        return mag * jnp.cos(expo * ldi_row), mag * jnp.sin(expo * ldi_row)

    mag = jnp.exp(ldr_col)
    ab_re, ab_im = mag * jnp.cos(ldi_col), mag * jnp.sin(ldi_col)
    den = lr_col * lr_col + li_col * li_col
    nr = ab_re - 1.0
    f_re = (nr * lr_col + ab_im * li_col) / den
    f_im = (ab_im * lr_col - nr * li_col) / den
    bt_re, bt_im = bt_re_ref[0, 0], bt_im_ref[0, 0]
    bbt_re = f_re * bt_re - f_im * bt_im
    bbt_im = f_re * bt_im + f_im * bt_re
    bb_re, bb_im = bbt_re[:, :i_dim], bbt_im[:, :i_dim]

    ct_re, ct_im = ct_re_ref[0, 0], ct_im_ref[0, 0]
    tok = (lax.broadcasted_iota(jnp.int32, (rows, 1), 0) // i_dim).astype(F32)

    e_re, e_im = power_rows(tok)
    ck_re = ct_re * e_re - ct_im * e_im
    ck_im = ct_re * e_im + ct_im * e_re
    k_ref[0, 0] = _dot(ck_re, bb_re) - _dot(ck_im, bb_im)

    f_exp = jnp.where(backward, float(chunk) - tok, tok + 1.0)
    e_re, e_im = power_rows(f_exp)
    g_re_ref[0, 0] = ct_re * e_re - ct_im * e_im
    g_im_ref[0, 0] = ct_re * e_im + ct_im * e_re

    tok_l = (lax.broadcasted_iota(jnp.int32, (1, rows), 1) // i_dim).astype(F32)
    e_exp = jnp.where(backward, tok_l, float(chunk - 1) - tok_l)
    magw = jnp.exp(ldr_col * e_exp)
    ew_re, ew_im = magw * jnp.cos(ldi_col * e_exp), magw * jnp.sin(ldi_col * e_exp)
    w_re_ref[0, 0] = ew_re * bbt_re - ew_im * bbt_im
    w_im_ref[0, 0] = ew_re * bbt_im + ew_im * bbt_re

    full = jnp.full((1, 1), float(chunk), F32)
    a_re, a_im = power_rows(full)
    a_ref[0, 0] = jnp.concatenate([a_re, a_im], axis=0)


def _s5_operators(lam_re, lam_im, log_step, b_re, b_im, c_re, c_im):
    _, g, p = lam_re.shape
    i = b_re.shape[-1]
    t = STEP_CHUNK
    rows = t * i
    lam_row = jnp.stack([lam_re, lam_im], axis=2)
    lam_col = jnp.stack([lam_re, lam_im], axis=3)
    ls = log_step.reshape(2, g, 1, 1)
    bt_re = jnp.tile(b_re, (1, 1, 1, t))
    bt_im = jnp.tile(b_im, (1, 1, 1, t))
    ct_re = jnp.tile(c_re, (1, 1, t, 1))
    ct_im = jnp.tile(c_im, (1, 1, t, 1))
    blk = lambda *shape: pl.BlockSpec((1, 1) + shape, lambda dr, gi: (dr, gi, 0, 0))
    k, w_re, w_im, g_re, g_im, a = pl.pallas_call(
        functools.partial(_s5_param_kernel, t),
        grid=(2, g),
        in_specs=[blk(2, p), blk(p, 2), blk(1, 1), blk(p, rows), blk(p, rows),
                  blk(rows, p), blk(rows, p)],
        out_specs=[blk(rows, i), blk(p, rows), blk(p, rows), blk(rows, p), blk(rows, p), blk(2, p)],
        out_shape=[jax.ShapeDtypeStruct((2, g, rows, i), F32),
                   jax.ShapeDtypeStruct((2, g, p, rows), F32),
                   jax.ShapeDtypeStruct((2, g, p, rows), F32),
                   jax.ShapeDtypeStruct((2, g, rows, p), F32),
                   jax.ShapeDtypeStruct((2, g, rows, p), F32),
                   jax.ShapeDtypeStruct((2, g, 2, p), F32)],
        compiler_params=_cparams("arbitrary", "arbitrary"),
        name="s5_operators",
    )(lam_row, lam_col, ls, bt_re, bt_im, ct_re, ct_im)

    k5 = k.reshape(2, g, t, i, i)
    s_idx = np.arange(t)[:, None]
    t_idx = np.arange(t)[None, :]

    def toeplitz(kd, lag):
        m = kd[:, np.clip(lag, 0, t - 1)]
        m = jnp.where((lag >= 0)[None, :, :, None, None], m, 0.0)
        return m.transpose(0, 1, 4, 2, 3).reshape(g, rows, rows)

    m = jnp.stack([toeplitz(k5[0], t_idx - s_idx), toeplitz(k5[1], s_idx - t_idx)])
    return m, w_re, w_im, g_re, g_im, a


def _s5_scan_kernel(bsz, u_ref, d_ref, m_ref, w_re_ref, w_im_ref, g_re_ref, g_im_ref, a_ref,
                    h0_re_ref, h0_im_ref, y_ref, hf_re_ref, hf_im_ref,
                    z_re_scr, z_im_scr, hp_re_scr, hp_im_scr):
    backward = pl.program_id(1) == 1
    u = u_ref[0]
    n_chunks = u.shape[0] // bsz
    z_re_scr[...] = _dot_nt(u, w_re_ref[0, 0])
    z_im_scr[...] = _dot_nt(u, w_im_ref[0, 0])
    a = a_ref[0, 0]
    a_re, a_im = a[0:1, :], a[1:2, :]

    def chunk_step(k, carry):
        h_re, h_im = carry
        c = jnp.where(backward, n_chunks - 1 - k, k)
        rows = pl.ds(pl.multiple_of(c * bsz, bsz), bsz)
        hp_re_scr[rows, :] = h_re
        hp_im_scr[rows, :] = h_im
        n_re = a_re * h_re - a_im * h_im + z_re_scr[rows, :]
        n_im = a_re * h_im + a_im * h_re + z_im_scr[rows, :]
        return n_re, n_im

    h_re, h_im = lax.fori_loop(0, n_chunks, chunk_step, (h0_re_ref[0, 0], h0_im_ref[0, 0]))
    hf_re_ref[0, 0] = h_re
    hf_im_ref[0, 0] = h_im

    y = (_dot(u, m_ref[0, 0]) + _dot_nt(hp_re_scr[...], g_re_ref[0, 0])
         - _dot_nt(hp_im_scr[...], g_im_ref[0, 0]))

    @pl.when(jnp.logical_not(backward))
    def _():
        y_ref[0] = d_ref[0] * u + y

    @pl.when(backward)
    def _():
        y_ref[0] = y_ref[0] + y


def _s5_bidir(u, ops, d_skip, h0):
    m, w_re, w_im, g_re, g_im, a = ops
    bsz, n, width = u.shape
    _, g, p, rows = w_re.shape
    t = STEP_CHUNK
    i = rows // t
    nc = n // t
    r = nc * bsz
    ug = u.reshape(bsz, nc, t, g, i).transpose(3, 1, 0, 2, 4).reshape(g, r, rows)
    dg = jnp.tile(d_skip.reshape(g, 1, i), (1, 1, t))
    if h0 is None:
        h0 = (jnp.zeros((2, g, bsz, p), F32),) * 2
    per_g = lambda *shape: pl.BlockSpec((1,) + shape, lambda gi, dr: (gi, 0, 0))
    per_dg = lambda *shape: pl.BlockSpec((1, 1) + shape, lambda gi, dr: (dr, gi, 0, 0))
    y, hf_re, hf_im = pl.pallas_call(
        functools.partial(_s5_scan_kernel, bsz),
        grid=(g, 2),
        in_specs=[per_g(r, rows), per_g(1, rows), per_dg(rows, rows), per_dg(p, rows), per_dg(p, rows),
                  per_dg(rows, p), per_dg(rows, p), per_dg(2, p), per_dg(bsz, p), per_dg(bsz, p)],
        out_specs=[per_g(r, rows), per_dg(bsz, p), per_dg(bsz, p)],
        out_shape=[jax.ShapeDtypeStruct((g, r, rows), F32),
                   jax.ShapeDtypeStruct((2, g, bsz, p), F32),
                   jax.ShapeDtypeStruct((2, g, bsz, p), F32)],
        scratch_shapes=[pltpu.VMEM((r, p), F32)] * 4,
        compiler_params=_cparams("arbitrary", "arbitrary"),
        name="s5_scan",
    )(ug, dg, m, w_re, w_im, g_re, g_im, a, h0[0], h0[1])
    y = y.reshape(g, nc, bsz, t, i).transpose(2, 1, 3, 0, 4).reshape(bsz, n, width)
    return y, (hf_re, hf_im)


def _even_post_kernel(seg_len, alpha, ys_ref, pc_ref, h_ref, g1_ref, wout_ref, gluw_ref, glub_ref,
                      cw_ref, cb_ref, lng_ref, lnb_ref, o_ref):
    ys = ys_ref[0]
    tl, sw = ys.shape
    g = _gelu(ys)
    ya = g * jax.nn.sigmoid(_dot(g, gluw_ref[...]) + glub_ref[...])
    cwid = pc_ref.shape[-1] // 3
    z = pc_ref[0, :, cwid:2 * cwid] * pc_ref[0, :, 2 * cwid:]
    pos = lax.broadcasted_iota(jnp.int32, (tl, 1), 0) % seg_len
    z_prev = jnp.where(pos == 0, 0.0, pltpu.roll(z, 1, axis=0))
    z_next = jnp.where(pos == seg_len - 1, 0.0, pltpu.roll(z, tl - 1, axis=0))
    cw = cw_ref[...]
    conv = cw[0:1, :] * z_prev + cw[1:2, :] * z + cw[2:3, :] * z_next + cb_ref[...]
    yb = pc_ref[0, :, :cwid] * conv
    out = _dot(ya, wout_ref[:sw, :]) + _dot(yb, wout_ref[sw:, :])
    h = h_ref[0]
    o_ref[0] = _layer_norm(alpha * h + g1_ref[0] * out, lng_ref[...], lnb_ref[...])


def _even_post(y_ssm, p_conv, h, g1, w_out, glu_w, glu_b, conv_w, conv_b, ln_g, ln_b, seg_len, alpha):
    bsz, n, d = h.shape
    sw = y_ssm.shape[-1]
    cw3 = p_conv.shape[-1]
    cwid = cw3 // 3
    tl = _token_tile(n, 512)
    assert tl % seg_len == 0
    tok = lambda b, j: (b, j, 0)
    mod = lambda b, j: (b, 0, 0)
    full = lambda *shape: pl.BlockSpec(shape, lambda b, j: (0,) * len(shape))
    return pl.pallas_call(
        functools.partial(_even_post_kernel, seg_len, alpha),
        grid=(bsz, n // tl),
        in_specs=[pl.BlockSpec((1, tl, sw), tok), pl.BlockSpec((1, tl, cw3), tok),
                  pl.BlockSpec((1, tl, d), tok), pl.BlockSpec((1, 1, d), mod),
                  full(sw + cwid, d), full(sw, sw), full(1, sw), full(3, cwid), full(1, cwid),
                  full(1, d), full(1, d)],
        out_specs=pl.BlockSpec((1, tl, d), tok),
        out_shape=jax.ShapeDtypeStruct((bsz, n, d), F32),
        compiler_params=_cparams("arbitrary", "arbitrary"),
        name="even_post",
    )(y_ssm, p_conv, h, g1, w_out, glu_w, glu_b.reshape(1, sw), conv_w, conv_b.reshape(1, cwid),
      ln_g.reshape(1, d), ln_b.reshape(1, d))


def _pool_matrices(seg_len, tile):
    pos = np.arange(tile) % seg_len
    seg = np.arange(tile) // seg_len
    mats = []
    for w in POOL_WINDOWS:
        lo = np.clip(pos - w // 2, 0, seg_len - 1)
        hi = np.clip(pos + w // 2 - 1, 0, seg_len - 1)
        cnt = (hi - lo + 1).astype(np.float64)
        inside = ((seg[:, None] == seg[None, :]) & (pos[None, :] >= lo[:, None])
                  & (pos[None, :] <= hi[:, None]))
        mats.append(inside / cnt[:, None] - np.eye(tile))
    return np.stack(mats).astype(np.float32)


def _pool_kernel(alpha, h_ref, sc_ref, sh_ref, g1_ref, pm_ref, pw_ref, ps_ref, lng_ref, lnb_ref, o_ref):
    h = h_ref[0]
    a = h * (1.0 + sc_ref[0]) + sh_ref[0]
    n_win, pg, _ = pw_ref.shape
    ys = []
    for gi in range(n_win):
        z = _dot(pm_ref[gi], a[:, gi * pg:(gi + 1) * pg])
        ys.append(_dot(z, pw_ref[gi]))
    y = jnp.concatenate(ys, axis=-1) * ps_ref[...]
    o_ref[0] = _layer_norm(alpha * h + g1_ref[0] * y, lng_ref[...], lnb_ref[...])


def _pool_mixer(h, sc, sh, g1, pool_w, pool_scale, ln_g, ln_b, seg_len, alpha):
    bsz, n, d = h.shape
    tl = _token_tile(n, 256)
    assert tl % seg_len == 0
    pm = jnp.asarray(_pool_matrices(seg_len, tl))
    n_win, pg, _ = pool_w.shape
    tok = lambda b, j: (b, j, 0)
    mod = lambda b, j: (b, 0, 0)
    full = lambda *shape: pl.BlockSpec(shape, lambda b, j: (0,) * len(shape))
    return pl.pallas_call(
        functools.partial(_pool_kernel, alpha),
        grid=(bsz, n // tl),
        in_specs=[pl.BlockSpec((1, tl, d), tok), pl.BlockSpec((1, 1, d), mod), pl.BlockSpec((1, 1, d), mod),
                  pl.BlockSpec((1, 1, d), mod), full(n_win, tl, tl), full(n_win, pg, pg), full(1, d),
                  full(1, d), full(1, d)],
        out_specs=pl.BlockSpec((1, tl, d), tok),
        out_shape=jax.ShapeDtypeStruct((bsz, n, d), F32),
        compiler_params=_cparams("arbitrary", "arbitrary"),
        name="pool_mixer",
    )(h, sc, sh, g1, pm, pool_w, pool_scale.reshape(1, d), ln_g.reshape(1, d), ln_b.reshape(1, d))


def _top_rows(s, n_top):
    n = s.shape[0]
    rid = lax.broadcasted_iota(jnp.int32, s.shape, 0).astype(F32)
    vals, rows = [], []
    for _ in range(n_top):
        m = jnp.max(s, axis=0, keepdims=True)
        sel = jnp.min(jnp.where(s == m, rid, float(n)), axis=0, keepdims=True)
        vals.append(m)
        rows.append(sel)
        s = jnp.where(rid == sel, -jnp.inf, s)
    return jnp.concatenate(vals, axis=0), jnp.concatenate(rows, axis=0)


def _candidate_slabs(k_top):
    slabs = []
    for a in range(k_top):
        n_valid = k_top // (a + 1)
        if n_valid == 1:
            slabs.append((a, k_top, 1, 1))
            break
        slabs.append((a, a + 1, -(-n_valid // 8) * 8, n_valid))
    return slabs


def _peer_route_kernel(n_heads, h_ref, sc_ref, sh_ref, wq_ref, keys_ref, idx_ref, gate_ref):
    xin = (h_ref[0] * (1.0 + sc_ref[0]) + sh_ref[0]).astype(jnp.bfloat16)
    n_keys, half = keys_ref.shape[1], keys_ref.shape[2]
    k_top = PEER_TOPK
    slabs = _candidate_slabs(k_top)
    for head in range(n_heads):
        tops = []
        for side in range(2):
            col = (head * 2 + side) * half
            q = jnp.dot(xin, wq_ref[:, col:col + half], preferred_element_type=F32)
            s = _dot_nt(keys_ref[side], q)
            tops.append(_top_rows(s, k_top))
        (v0, i0), (v1, i1) = tops
        cs, ci = [], []
        for a_lo, a_hi, n_b, n_valid in slabs:
            b_hi = 1 if n_b == 1 else n_b
            sl = v0[a_lo:a_hi] + v1[0:b_hi]
            if n_valid < n_b:
                sl = jnp.where(lax.broadcasted_iota(jnp.int32, sl.shape, 0) < n_valid, sl, -jnp.inf)
            cs.append(sl)
            ci.append(i0[a_lo:a_hi] * float(n_keys) + i1[0:b_hi])
        cand_s = jnp.concatenate(cs, axis=0)
        cand_i = jnp.concatenate(ci, axis=0)
        n_cand = cand_s.shape[0]
        rid = lax.broadcasted_iota(jnp.int32, cand_s.shape, 0).astype(F32)
        best_s, best_i = [], []
        s = cand_s
        for _ in range(k_top):
            m = jnp.max(s, axis=0, keepdims=True)
            sel = jnp.min(jnp.where(s == m, rid, float(n_cand)), axis=0, keepdims=True)
            hit = rid == sel
            best_s.append(m)
            best_i.append(jnp.sum(jnp.where(hit, cand_i, 0.0), axis=0, keepdims=True))
            s = jnp.where(hit, -jnp.inf, s)
        bs = jnp.concatenate(best_s, axis=0)
        e = jnp.exp(bs - bs[0:1, :])
        gate = e / jnp.sum(e, axis=0, keepdims=True)
        idx_ref[0, head * k_top:(head + 1) * k_top, :] = jnp.concatenate(best_i, axis=0).astype(jnp.int32)
        gate_ref[0, head * k_top:(head + 1) * k_top, :] = gate


def _peer_route(h, sc, sh, wq, keys):
    bsz, n, d = h.shape
    _, n_keys, half = keys.shape
    n_heads = wq.shape[1] // (2 * half)
    tt = _token_tile(n, 256)
    hk = n_heads * PEER_TOPK
    mod = lambda b, j: (b, 0, 0)
    return pl.pallas_call(
        functools.partial(_peer_route_kernel, n_heads),
        grid=(bsz, n // tt),
        in_specs=[pl.BlockSpec((1, tt, d), lambda b, j: (b, j, 0)),
                  pl.BlockSpec((1, 1, d), mod), pl.BlockSpec((1, 1, d), mod),
                  pl.BlockSpec(wq.shape, lambda b, j: (0, 0)),
                  pl.BlockSpec(keys.shape, lambda b, j: (0, 0, 0))],
        out_specs=[pl.BlockSpec((1, hk, tt), lambda b, j: (b, 0, j)),
                   pl.BlockSpec((1, hk, tt), lambda b, j: (b, 0, j))],
        out_shape=[jax.ShapeDtypeStruct((bsz, hk, n), jnp.int32),
                   jax.ShapeDtypeStruct((bsz, hk, n), F32)],
        compiler_params=_cparams("arbitrary", "arbitrary"),
        name="peer_route",
    )(h, sc, sh, wq.astype(jnp.bfloat16), keys)


ISSUE_UNROLL = 8


def _expert_pitch(rows):
    tiles = -(-rows // V7X_SUBLANES)
    return V7X_SUBLANES * (tiles if tiles % 2 else tiles + 1)


def _peer_expert_kernel(alpha, pitch, idx_ref, gate_ref, h_ref, sc_ref, sh_ref, g2_ref, lng_ref, lnb_ref,
                        uv_hbm, o_ref, uvbuf, xin_scr, f_scr, sems):
    tb, n_sel = gate_ref.shape[1], gate_ref.shape[2]
    h = h_ref[0]
    d = h.shape[-1]
    lanes = uvbuf.shape[-1]
    ru = d // lanes
    rows = 2 * ru
    slot_rows = n_sel * pitch
    xin_scr[...] = h * (1.0 + sc_ref[0]) + sh_ref[0]

    def issue(t, slot):
        def body(j, carry):
            k0 = j * ISSUE_UNROLL
            dst0 = slot * slot_rows + k0 * pitch
            for r in range(ISSUE_UNROLL):
                e = idx_ref[t * n_sel + k0 + r]
                src = uv_hbm.at[pl.ds(pl.multiple_of(e * rows, rows), rows), :]
                dst = uvbuf.at[pl.ds(pl.multiple_of(dst0 + r * pitch, V7X_SUBLANES), rows), :]
                pltpu.make_async_copy(src, dst, sems.at[slot]).start()
            return carry
        lax.fori_loop(0, n_sel // ISSUE_UNROLL, body, 0)

    def wait(slot):
        total = n_sel * rows
        pltpu.make_async_copy(uv_hbm.at[pl.ds(0, total), :], uvbuf.at[pl.ds(0, total), :], sems.at[slot]).wait()

    eye = (lax.broadcasted_iota(jnp.int32, (n_sel, n_sel), 0)
           == lax.broadcasted_iota(jnp.int32, (n_sel, n_sel), 1))

    issue(0, 0)

    def token(t, carry):
        slot = t % 2

        @pl.when(t + 1 < tb)
        def _():
            issue(t + 1, 1 - slot)

        wait(slot)
        base = slot * slot_rows
        x = xin_scr[pl.ds(t, 1), :]
        dots = None
        for s in range(ru):
            u_s = uvbuf[pl.ds(base + s, n_sel, stride=pitch), :]
            part = u_s * x[:, s * lanes:(s + 1) * lanes]
            dots = part if dots is None else dots + part
        act = _gelu(jnp.sum(dots, axis=1, keepdims=True))
        g_row = gate_ref[0, pl.ds(t, 1), :]
        g_col = jnp.sum(jnp.where(eye, g_row, 0.0), axis=1, keepdims=True)
        coef = g_col * act
        f_parts = [jnp.sum(coef * uvbuf[pl.ds(base + ru + s, n_sel, stride=pitch), :], axis=0, keepdims=True)
                   for s in range(ru)]
        f_scr[pl.ds(t, 1), :] = jnp.concatenate(f_parts, axis=1)
        return carry

    lax.fori_loop(0, tb, token, 0)
    o_ref[0] = _layer_norm(alpha * h + g2_ref[0] * f_scr[...], lng_ref[...], lnb_ref[...])


def _expert_table(u_tab, v_tab):
    n_exp, d = u_tab.shape
    ru = d // V7X_LANES
    uv = jnp.concatenate([u_tab.reshape(n_exp, ru, V7X_LANES), v_tab.reshape(n_exp, ru, V7X_LANES)], axis=1)
    return uv.reshape(n_exp * 2 * ru, V7X_LANES)


def _peer_experts(idx, gate, h, sc, sh, g2, ln_g, ln_b, uv_tab, alpha):
    bsz, n, d = h.shape
    n_sel = idx.shape[-1]
    assert n_sel % ISSUE_UNROLL == 0 and d % V7X_LANES == 0
    pitch = _expert_pitch(2 * d // V7X_LANES)
    tb = _token_tile(n, 64)
    nb = n // tb
    tok = lambda b, j: (b, j, 0)
    mod = lambda b, j: (b, 0, 0)
    full = lambda *shape: pl.BlockSpec(shape, lambda b, j: (0,) * len(shape))
    return pl.pallas_call(
        functools.partial(_peer_expert_kernel, alpha, pitch),
        grid=(bsz, nb),
        in_specs=[pl.BlockSpec((tb * n_sel,), lambda b, j: (b * nb + j,), memory_space=pltpu.SMEM),
                  pl.BlockSpec((1, tb, n_sel), tok), pl.BlockSpec((1, tb, d), tok),
                  pl.BlockSpec((1, 1, d), mod), pl.BlockSpec((1, 1, d), mod), pl.BlockSpec((1, 1, d), mod),
                  full(1, d), full(1, d),
                  pl.BlockSpec(memory_space=pl.ANY)],
        out_specs=pl.BlockSpec((1, tb, d), tok),
        out_shape=jax.ShapeDtypeStruct((bsz, n, d), F32),
        scratch_shapes=[pltpu.VMEM((2 * n_sel * pitch, V7X_LANES), F32),
                        pltpu.VMEM((tb, d), F32), pltpu.VMEM((tb, d), F32),
                        pltpu.SemaphoreType.DMA((2,))],
        compiler_params=_cparams("arbitrary", "arbitrary"),
        name="peer_experts",
    )(idx.reshape(bsz * n * n_sel), gate, h, sc, sh, g2, ln_g.reshape(1, d), ln_b.reshape(1, d), uv_tab)


def _peer_block(h, sc2, sh2, g2, wq, keys, uv_tab, ln_g, ln_b, alpha):
    idx, gate = _peer_route(h, sc2, sh2, wq, keys)
    idx = idx.transpose(0, 2, 1)
    gate = gate.transpose(0, 2, 1)
    return _peer_experts(idx, gate, h, sc2, sh2, g2, ln_g, ln_b, uv_tab, alpha)


def kernel(x, c, ctx, c_ctx, ada_w, ada_b, ln_mix_g, ln_mix_b, ln_ffn_g, ln_ffn_b, even_w_in, even_w_out, ssm_lam_re, ssm_lam_im, ssm_log_step, ssm_b_re, ssm_b_im, ssm_c_re, ssm_c_im, ssm_d, glu_w, glu_b, conv_w, conv_b, pool_w, pool_scale, peer_wq, peer_keys, peer_u, peer_v):
    depth, d, _ = ada_w.shape
    bsz, seq, _ = x.shape
    ctx_len = ctx.shape[1]
    ssm_w = ssm_d.shape[-1]
    alpha = (2.0 * depth) ** 0.25

    pad = (-(bsz + 1)) % 8
    c_all = jnp.concatenate([c, c_ctx[None, :], jnp.zeros((pad, d), F32)], axis=0)
    mod = _ada_modulation(c_all, ada_w, ada_b)

    def six(rows):
        return [rows[:, None, k * d:(k + 1) * d] for k in range(6)]

    h_lat, h_ctx = x, ctx
    for layer in range(depth):
        ctx_out = any(j > layer and j % 2 == 0 for j in range(depth))
        even = layer % 2 == 0
        sh1, sc1, g1, sh2, sc2, g2 = six(mod[layer, :bsz])
        need_ctx = ctx_out or even
        if need_ctx:
            csh1, csc1, cg1, csh2, csc2, cg2 = six(jnp.broadcast_to(mod[layer, bsz:bsz + 1], (bsz, 6 * d)))
        lmg, lmb = ln_mix_g[layer], ln_mix_b[layer]
        if even:
            e = layer // 2
            ops = _s5_operators(ssm_lam_re[e], ssm_lam_im[e], ssm_log_step[e], ssm_b_re[e], ssm_b_im[e],
                                ssm_c_re[e], ssm_c_im[e])
            post = functools.partial(_even_post, w_out=even_w_out[e], glu_w=glu_w[e], glu_b=glu_b[e],
                                     conv_w=conv_w[e], conv_b=conv_b[e], ln_g=lmg, ln_b=lmb, alpha=alpha)
            ps_ctx, pc_ctx = _in_projection(h_ctx, csc1, csh1, even_w_in[e], ssm_w)
            y_ctx, finals = _s5_bidir(ps_ctx, ops, ssm_d[e], None)
            ps_lat, pc_lat = _in_projection(h_lat, sc1, sh1, even_w_in[e], ssm_w)
            y_lat, _ = _s5_bidir(ps_lat, ops, ssm_d[e], finals)
            h_lat = post(y_lat, pc_lat, h_lat, g1, seg_len=GRID_W)
            if ctx_out:
                h_ctx = post(y_ctx, pc_ctx, h_ctx, cg1, seg_len=ctx_len)
        else:
            o = layer // 2
            h_lat = _pool_mixer(h_lat, sc1, sh1, g1, pool_w[o], pool_scale[o], lmg, lmb, GRID_W, alpha)
            if ctx_out:
                h_ctx = _pool_mixer(h_ctx, csc1, csh1, cg1, pool_w[o], pool_scale[o], lmg, lmb, ctx_len, alpha)
        uv_tab = _expert_table(peer_u[layer], peer_v[layer])
        peer = functools.partial(_peer_block, wq=peer_wq[layer], keys=peer_keys[layer], uv_tab=uv_tab,
                                 ln_g=ln_ffn_g[layer], ln_b=ln_ffn_b[layer], alpha=alpha)
        h_lat = peer(h_lat, sc2, sh2, g2)
        if ctx_out:
            h_ctx = peer(h_ctx, csc2, csh2, cg2)
    return h_lat
```

```python
import functools
import math

import numpy as np
import jax
import jax.numpy as jnp
from jax import lax
from jax.experimental import pallas as pl
from jax.experimental.pallas import tpu as pltpu

F32 = jnp.float32
HI = lax.Precision.HIGHEST

GRID_W = 64
POOL_WINDOWS = (2, 4, 8, 16)
PEER_TOPK = 16
LN_EPS = 1e-5
STEP_CHUNK = 64
VMEM_LIMIT_V7X = 56 * 1024 * 1024
V7X_SUBLANES, V7X_LANES = 8, 128
NT_DIMS = (((1,), (1,)), ((), ()))


def _cparams(*sem):
    return pltpu.CompilerParams(dimension_semantics=sem, vmem_limit_bytes=VMEM_LIMIT_V7X)


def _dot(a, b):
    return jnp.dot(a, b, preferred_element_type=F32, precision=HI)


def _dot_nt(a, b):
    return lax.dot_general(a, b, NT_DIMS, preferred_element_type=F32, precision=HI)


def _gelu(x):
    return 0.5 * x * (1.0 + lax.erf(x * (1.0 / math.sqrt(2.0))))


def _layer_norm(x, g, b):
    mu = jnp.mean(x, axis=-1, keepdims=True)
    xc = x - mu
    var = jnp.mean(xc * xc, axis=-1, keepdims=True)
    return xc * lax.rsqrt(var + LN_EPS) * g + b


def _token_tile(n, cap):
    t = min(n, cap)
    assert n % t == 0
    return t


def _ada_kernel(c_ref, w_ref, b_ref, o_ref):
    c = c_ref[...]
    s = c * jax.nn.sigmoid(c)
    o_ref[0] = _dot(s, w_ref[0]) + b_ref[0]


def _ada_modulation(c_all, ada_w, ada_b):
    depth, d, e = ada_w.shape
    rows = c_all.shape[0]
    te = e // 6
    return pl.pallas_call(
        _ada_kernel,
        grid=(depth, e // te),
        in_specs=[pl.BlockSpec((rows, d), lambda l, j: (0, 0)),
                  pl.BlockSpec((1, d, te), lambda l, j: (l, 0, j)),
                  pl.BlockSpec((1, 1, te), lambda l, j: (l, 0, j))],
        out_specs=pl.BlockSpec((1, rows, te), lambda l, j: (l, 0, j)),
        out_shape=jax.ShapeDtypeStruct((depth, rows, e), F32),
        compiler_params=_cparams("arbitrary", "arbitrary"),
        name="ada_modulation",
    )(c_all, ada_w, ada_b.reshape(depth, 1, e))


def _inproj_kernel(ssm_w, h_ref, sc_ref, sh_ref, w_ref, ps_ref, pc_ref):
    a = h_ref[0] * (1.0 + sc_ref[0]) + sh_ref[0]
    ps_ref[0] = _dot(a, w_ref[:, :ssm_w])
    pc_ref[0] = _dot(a, w_ref[:, ssm_w:])


def _in_projection(h, sc, sh, w_in, ssm_w):
    bsz, n, d = h.shape
    e = w_in.shape[1]
    tl = _token_tile(n, 512)
    tok = lambda b, j: (b, j, 0)
    mod = lambda b, j: (b, 0, 0)
    return pl.pallas_call(
        functools.partial(_inproj_kernel, ssm_w),
        grid=(bsz, n // tl),
        in_specs=[pl.BlockSpec((1, tl, d), tok),
                  pl.BlockSpec((1, 1, d), mod),
                  pl.BlockSpec((1, 1, d), mod),
                  pl.BlockSpec((d, e), lambda b, j: (0, 0))],
        out_specs=[pl.BlockSpec((1, tl, ssm_w), tok),
                   pl.BlockSpec((1, tl, e - ssm_w), tok)],
        out_shape=[jax.ShapeDtypeStruct((bsz, n, ssm_w), F32),
                   jax.ShapeDtypeStruct((bsz, n, e - ssm_w), F32)],
        compiler_params=_cparams("arbitrary", "arbitrary"),
        name="even_in_projection",
    )(h, sc, sh, w_in)


def _s5_param_kernel(chunk, lam_row_ref, lam_col_ref, ls_ref, bt_re_ref, bt_im_ref,
                     ct_re_ref, ct_im_ref, k_ref, w_re_ref, w_im_ref, g_re_ref, g_im_ref, a_ref):
    backward = pl.program_id(0) == 1
    p_dim = lam_row_ref.shape[-1]
    i_dim = k_ref.shape[-1]
    rows = chunk * i_dim
    step = jnp.exp(ls_ref[0, 0])
    lam_row = lam_row_ref[0, 0]
    lr_row, li_row = lam_row[0:1, :], lam_row[1:2, :]
    lam_col = lam_col_ref[0, 0]
    lr_col, li_col = lam_col[:, 0:1], lam_col[:, 1:2]
    ldr_row, ldi_row = lr_row * step, li_row * step
    ldr_col, ldi_col = lr_col * step, li_col * step

    def power_rows(expo):
        mag = jnp.exp(expo * ldr_row)
        return mag * jnp.cos(expo * ldi_row), mag * jnp.sin(expo * ldi_row)

    mag = jnp.exp(ldr_col)
    ab_re, ab_im = mag * jnp.cos(ldi_col), mag * jnp.sin(ldi_col)
    den = lr_col * lr_col + li_col * li_col
    nr = ab_re - 1.0
    f_re = (nr * lr_col + ab_im * li_col) / den
    f_im = (ab_im * lr_col - nr * li_col) / den
    bt_re, bt_im = bt_re_ref[0, 0], bt_im_ref[0, 0]
    bbt_re = f_re * bt_re - f_im * bt_im
    bbt_im = f_re * bt_im + f_im * bt_re
    bb_re, bb_im = bbt_re[:, :i_dim], bbt_im[:, :i_dim]

    ct_re, ct_im = ct_re_ref[0, 0], ct_im_ref[0, 0]
    tok = (lax.broadcasted_iota(jnp.int32, (rows, 1), 0) // i_dim).astype(F32)

    e_re, e_im = power_rows(tok)
    ck_re = ct_re * e_re - ct_im * e_im
    ck_im = ct_re * e_im + ct_im * e_re
    k_ref[0, 0] = _dot(ck_re, bb_re) - _dot(ck_im, bb_im)

    f_exp = jnp.where(backward, float(chunk) - tok, tok + 1.0)
    e_re, e_im = power_rows(f_exp)
    g_re_ref[0, 0] = ct_re * e_re - ct_im * e_im
    g_im_ref[0, 0] = ct_re * e_im + ct_im * e_re

    tok_l = (lax.broadcasted_iota(jnp.int32, (1, rows), 1) // i_dim).astype(F32)
    e_exp = jnp.where(backward, tok_l, float(chunk - 1) - tok_l)
    magw = jnp.exp(ldr_col * e_exp)
    ew_re, ew_im = magw * jnp.cos(ldi_col * e_exp), magw * jnp.sin(ldi_col * e_exp)
    w_re_ref[0, 0] = ew_re * bbt_re - ew_im * bbt_im
    w_im_ref[0, 0] = ew_re * bbt_im + ew_im * bbt_re

    full = jnp.full((1, 1), float(chunk), F32)
    a_re, a_im = power_rows(full)
    a_ref[0, 0] = jnp.concatenate([a_re, a_im], axis=0)


def _s5_operators(lam_re, lam_im, log_step, b_re, b_im, c_re, c_im):
    _, g, p = lam_re.shape
    i = b_re.shape[-1]
    t = STEP_CHUNK
    rows = t * i
    lam_row = jnp.stack([lam_re, lam_im], axis=2)
    lam_col = jnp.stack([lam_re, lam_im], axis=3)
    ls = log_step.reshape(2, g, 1, 1)
    bt_re = jnp.tile(b_re, (1, 1, 1, t))
    bt_im = jnp.tile(b_im, (1, 1, 1, t))
    ct_re = jnp.tile(c_re, (1, 1, t, 1))
    ct_im = jnp.tile(c_im, (1, 1, t, 1))
    blk = lambda *shape: pl.BlockSpec((1, 1) + shape, lambda dr, gi: (dr, gi, 0, 0))
    k, w_re, w_im, g_re, g_im, a = pl.pallas_call(
        functools.partial(_s5_param_kernel, t),
        grid=(2, g),
        in_specs=[blk(2, p), blk(p, 2), blk(1, 1), blk(p, rows), blk(p, rows),
                  blk(rows, p), blk(rows, p)],
        out_specs=[blk(rows, i), blk(p, rows), blk(p, rows), blk(rows, p), blk(rows, p), blk(2, p)],
        out_shape=[jax.ShapeDtypeStruct((2, g, rows, i), F32),
                   jax.ShapeDtypeStruct((2, g, p, rows), F32),
                   jax.ShapeDtypeStruct((2, g, p, rows), F32),
                   jax.ShapeDtypeStruct((2, g, rows, p), F32),
                   jax.ShapeDtypeStruct((2, g, rows, p), F32),
                   jax.ShapeDtypeStruct((2, g, 2, p), F32)],
        compiler_params=_cparams("arbitrary", "arbitrary"),
        name="s5_operators",
    )(lam_row, lam_col, ls, bt_re, bt_im, ct_re, ct_im)

    k5 = k.reshape(2, g, t, i, i)
    s_idx = np.arange(t)[:, None]
    t_idx = np.arange(t)[None, :]

    def toeplitz(kd, lag):
        m = kd[:, np.clip(lag, 0, t - 1)]
        m = jnp.where((lag >= 0)[None, :, :, None, None], m, 0.0)
        return m.transpose(0, 1, 4, 2, 3).reshape(g, rows, rows)

    m = jnp.stack([toeplitz(k5[0], t_idx - s_idx), toeplitz(k5[1], s_idx - t_idx)])
    return m, w_re, w_im, g_re, g_im, a


def _s5_scan_kernel(bsz, u_ref, d_ref, m_ref, w_re_ref, w_im_ref, g_re_ref, g_im_ref, a_ref,
                    h0_re_ref, h0_im_ref, y_ref, hf_re_ref, hf_im_ref,
                    z_re_scr, z_im_scr, hp_re_scr, hp_im_scr):
    backward = pl.program_id(1) == 1
    u = u_ref[0]
    n_chunks = u.shape[0] // bsz
    z_re_scr[...] = _dot_nt(u, w_re_ref[0, 0])
    z_im_scr[...] = _dot_nt(u, w_im_ref[0, 0])
    a = a_ref[0, 0]
    a_re, a_im = a[0:1, :], a[1:2, :]

    def chunk_step(k, carry):
        h_re, h_im = carry
        c = jnp.where(backward, n_chunks - 1 - k, k)
        rows = pl.ds(pl.multiple_of(c * bsz, bsz), bsz)
        hp_re_scr[rows, :] = h_re
        hp_im_scr[rows, :] = h_im
        n_re = a_re * h_re - a_im * h_im + z_re_scr[rows, :]
        n_im = a_re * h_im + a_im * h_re + z_im_scr[rows, :]
        return n_re, n_im

    h_re, h_im = lax.fori_loop(0, n_chunks, chunk_step, (h0_re_ref[0, 0], h0_im_ref[0, 0]))
    hf_re_ref[0, 0] = h_re
    hf_im_ref[0, 0] = h_im

    y = (_dot(u, m_ref[0, 0]) + _dot_nt(hp_re_scr[...], g_re_ref[0, 0])
         - _dot_nt(hp_im_scr[...], g_im_ref[0, 0]))

    @pl.when(jnp.logical_not(backward))
    def _():
        y_ref[0] = d_ref[0] * u + y

    @pl.when(backward)
    def _():
        y_ref[0] = y_ref[0] + y


def _s5_bidir(u, ops, d_skip, h0):
    m, w_re, w_im, g_re, g_im, a = ops
    bsz, n, width = u.shape
    _, g, p, rows = w_re.shape
    t = STEP_CHUNK
    i = rows // t
    nc = n // t
    r = nc * bsz
    ug = u.reshape(bsz, nc, t, g, i).transpose(3, 1, 0, 2, 4).reshape(g, r, rows)
    dg = jnp.tile(d_skip.reshape(g, 1, i), (1, 1, t))
    if h0 is None:
        h0 = (jnp.zeros((2, g, bsz, p), F32),) * 2
    per_g = lambda *shape: pl.BlockSpec((1,) + shape, lambda gi, dr: (gi, 0, 0))
    per_dg = lambda *shape: pl.BlockSpec((1, 1) + shape, lambda gi, dr: (dr, gi, 0, 0))
    y, hf_re, hf_im = pl.pallas_call(
        functools.partial(_s5_scan_kernel, bsz),
        grid=(g, 2),
        in_specs=[per_g(r, rows), per_g(1, rows), per_dg(rows, rows), per_dg(p, rows), per_dg(p, rows),
                  per_dg(rows, p), per_dg(rows, p), per_dg(2, p), per_dg(bsz, p), per_dg(bsz, p)],
        out_specs=[per_g(r, rows), per_dg(bsz, p), per_dg(bsz, p)],
        out_shape=[jax.ShapeDtypeStruct((g, r, rows), F32),
                   jax.ShapeDtypeStruct((2, g, bsz, p), F32),
                   jax.ShapeDtypeStruct((2, g, bsz, p), F32)],
        scratch_shapes=[pltpu.VMEM((r, p), F32)] * 4,
        compiler_params=_cparams("arbitrary", "arbitrary"),
        name="s5_scan",
    )(ug, dg, m, w_re, w_im, g_re, g_im, a, h0[0], h0[1])
    y = y.reshape(g, nc, bsz, t, i).transpose(2, 1, 3, 0, 4).reshape(bsz, n, width)
    return y, (hf_re, hf_im)


def _even_post_kernel(seg_len, alpha, ys_ref, pc_ref, h_ref, g1_ref, wout_ref, gluw_ref, glub_ref,
                      cw_ref, cb_ref, lng_ref, lnb_ref, o_ref):
    ys = ys_ref[0]
    tl, sw = ys.shape
    g = _gelu(ys)
    ya = g * jax.nn.sigmoid(_dot(g, gluw_ref[...]) + glub_ref[...])
    cwid = pc_ref.shape[-1] // 3
    z = pc_ref[0, :, cwid:2 * cwid] * pc_ref[0, :, 2 * cwid:]
    pos = lax.broadcasted_iota(jnp.int32, (tl, 1), 0) % seg_len
    z_prev = jnp.where(pos == 0, 0.0, pltpu.roll(z, 1, axis=0))
    z_next = jnp.where(pos == seg_len - 1, 0.0, pltpu.roll(z, tl - 1, axis=0))
    cw = cw_ref[...]
    conv = cw[0:1, :] * z_prev + cw[1:2, :] * z + cw[2:3, :] * z_next + cb_ref[...]
    yb = pc_ref[0, :, :cwid] * conv
    out = _dot(ya, wout_ref[:sw, :]) + _dot(yb, wout_ref[sw:, :])
    h = h_ref[0]
    o_ref[0] = _layer_norm(alpha * h + g1_ref[0] * out, lng_ref[...], lnb_ref[...])


def _even_post(y_ssm, p_conv, h, g1, w_out, glu_w, glu_b, conv_w, conv_b, ln_g, ln_b, seg_len, alpha):
    bsz, n, d = h.shape
    sw = y_ssm.shape[-1]
    cw3 = p_conv.shape[-1]
    cwid = cw3 // 3
    tl = _token_tile(n, 512)
    assert tl % seg_len == 0
    tok = lambda b, j: (b, j, 0)
    mod = lambda b, j: (b, 0, 0)
    full = lambda *shape: pl.BlockSpec(shape, lambda b, j: (0,) * len(shape))
    return pl.pallas_call(
        functools.partial(_even_post_kernel, seg_len, alpha),
        grid=(bsz, n // tl),
        in_specs=[pl.BlockSpec((1, tl, sw), tok), pl.BlockSpec((1, tl, cw3), tok),
                  pl.BlockSpec((1, tl, d), tok), pl.BlockSpec((1, 1, d), mod),
                  full(sw + cwid, d), full(sw, sw), full(1, sw), full(3, cwid), full(1, cwid),
                  full(1, d), full(1, d)],
        out_specs=pl.BlockSpec((1, tl, d), tok),
        out_shape=jax.ShapeDtypeStruct((bsz, n, d), F32),
        compiler_params=_cparams("arbitrary", "arbitrary"),
        name="even_post",
    )(y_ssm, p_conv, h, g1, w_out, glu_w, glu_b.reshape(1, sw), conv_w, conv_b.reshape(1, cwid),
      ln_g.reshape(1, d), ln_b.reshape(1, d))


def _pool_matrices(seg_len, tile):
    pos = np.arange(tile) % seg_len
    seg = np.arange(tile) // seg_len
    mats = []
    for w in POOL_WINDOWS:
        lo = np.clip(pos - w // 2, 0, seg_len - 1)
        hi = np.clip(pos + w // 2 - 1, 0, seg_len - 1)
        cnt = (hi - lo + 1).astype(np.float64)
        inside = ((seg[:, None] == seg[None, :]) & (pos[None, :] >= lo[:, None])
                  & (pos[None, :] <= hi[:, None]))
        mats.append(inside / cnt[:, None] - np.eye(tile))
    return np.stack(mats).astype(np.float32)


def _pool_kernel(alpha, h_ref, sc_ref, sh_ref, g1_ref, pm_ref, pw_ref, ps_ref, lng_ref, lnb_ref, o_ref):
    h = h_ref[0]
    a = h * (1.0 + sc_ref[0]) + sh_ref[0]
    n_win, pg, _ = pw_ref.shape
    ys = []
    for gi in range(n_win):
        z = _dot(pm_ref[gi], a[:, gi * pg:(gi + 1) * pg])
        ys.append(_dot(z, pw_ref[gi]))
    y = jnp.concatenate(ys, axis=-1) * ps_ref[...]
    o_ref[0] = _layer_norm(alpha * h + g1_ref[0] * y, lng_ref[...], lnb_ref[...])


def _pool_mixer(h, sc, sh, g1, pool_w, pool_scale, ln_g, ln_b, seg_len, alpha):
    bsz, n, d = h.shape
    tl = _token_tile(n, 256)
    assert tl % seg_len == 0
    pm = jnp.asarray(_pool_matrices(seg_len, tl))
    n_win, pg, _ = pool_w.shape
    tok = lambda b, j: (b, j, 0)
    mod = lambda b, j: (b, 0, 0)
    full = lambda *shape: pl.BlockSpec(shape, lambda b, j: (0,) * len(shape))
    return pl.pallas_call(
        functools.partial(_pool_kernel, alpha),
        grid=(bsz, n // tl),
        in_specs=[pl.BlockSpec((1, tl, d), tok), pl.BlockSpec((1, 1, d), mod), pl.BlockSpec((1, 1, d), mod),
                  pl.BlockSpec((1, 1, d), mod), full(n_win, tl, tl), full(n_win, pg, pg), full(1, d),
                  full(1, d), full(1, d)],
        out_specs=pl.BlockSpec((1, tl, d), tok),
        out_shape=jax.ShapeDtypeStruct((bsz, n, d), F32),
        compiler_params=_cparams("arbitrary", "arbitrary"),
        name="pool_mixer",
    )(h, sc, sh, g1, pm, pool_w, pool_scale.reshape(1, d), ln_g.reshape(1, d), ln_b.reshape(1, d))


def _top_rows(s, n_top):
    n = s.shape[0]
    rid = lax.broadcasted_iota(jnp.int32, s.shape, 0).astype(F32)
    vals, rows = [], []
    for _ in range(n_top):
        m = jnp.max(s, axis=0, keepdims=True)
        sel = jnp.min(jnp.where(s == m, rid, float(n)), axis=0, keepdims=True)
        vals.append(m)
        rows.append(sel)
        s = jnp.where(rid == sel, -jnp.inf, s)
    return jnp.concatenate(vals, axis=0), jnp.concatenate(rows, axis=0)


def _candidate_slabs(k_top):
    slabs = []
    for a in range(k_top):
        n_valid = k_top // (a + 1)
        if n_valid == 1:
            slabs.append((a, k_top, 1, 1))
            break
        slabs.append((a, a + 1, -(-n_valid // 8) * 8, n_valid))
    return slabs


def _peer_route_kernel(n_heads, h_ref, sc_ref, sh_ref, wq_ref, keys_ref, idx_ref, gate_ref):
    xin = (h_ref[0] * (1.0 + sc_ref[0]) + sh_ref[0]).astype(jnp.bfloat16)
    n_keys, half = keys_ref.shape[1], keys_ref.shape[2]
    k_top = PEER_TOPK
    slabs = _candidate_slabs(k_top)
    for head in range(n_heads):
        tops = []
        for side in range(2):
            col = (head * 2 + side) * half
            q = jnp.dot(xin, wq_ref[:, col:col + half], preferred_element_type=F32)
            s = _dot_nt(keys_ref[side], q)
            tops.append(_top_rows(s, k_top))
        (v0, i0), (v1, i1) = tops
        cs, ci = [], []
        for a_lo, a_hi, n_b, n_valid in slabs:
            b_hi = 1 if n_b == 1 else n_b
            sl = v0[a_lo:a_hi] + v1[0:b_hi]
            if n_valid < n_b:
                sl = jnp.where(lax.broadcasted_iota(jnp.int32, sl.shape, 0) < n_valid, sl, -jnp.inf)
            cs.append(sl)
            ci.append(i0[a_lo:a_hi] * float(n_keys) + i1[0:b_hi])
        cand_s = jnp.concatenate(cs, axis=0)
        cand_i = jnp.concatenate(ci, axis=0)
        n_cand = cand_s.shape[0]
        rid = lax.broadcasted_iota(jnp.int32, cand_s.shape, 0).astype(F32)
        best_s, best_i = [], []
        s = cand_s
        for _ in range(k_top):
            m = jnp.max(s, axis=0, keepdims=True)
            sel = jnp.min(jnp.where(s == m, rid, float(n_cand)), axis=0, keepdims=True)
            hit = rid == sel
            best_s.append(m)
            best_i.append(jnp.sum(jnp.where(hit, cand_i, 0.0), axis=0, keepdims=True))
            s = jnp.where(hit, -jnp.inf, s)
        bs = jnp.concatenate(best_s, axis=0)
        e = jnp.exp(bs - bs[0:1, :])
        gate = e / jnp.sum(e, axis=0, keepdims=True)
        idx_ref[0, head * k_top:(head + 1) * k_top, :] = jnp.concatenate(best_i, axis=0).astype(jnp.int32)
        gate_ref[0, head * k_top:(head + 1) * k_top, :] = gate


def _peer_route(h, sc, sh, wq, keys):
    bsz, n, d = h.shape
    _, n_keys, half = keys.shape
    n_heads = wq.shape[1] // (2 * half)
    tt = _token_tile(n, 256)
    hk = n_heads * PEER_TOPK
    mod = lambda b, j: (b, 0, 0)
    return pl.pallas_call(
        functools.partial(_peer_route_kernel, n_heads),
        grid=(bsz, n // tt),
        in_specs=[pl.BlockSpec((1, tt, d), lambda b, j: (b, j, 0)),
                  pl.BlockSpec((1, 1, d), mod), pl.BlockSpec((1, 1, d), mod),
                  pl.BlockSpec(wq.shape, lambda b, j: (0, 0)),
                  pl.BlockSpec(keys.shape, lambda b, j: (0, 0, 0))],
        out_specs=[pl.BlockSpec((1, hk, tt), lambda b, j: (b, 0, j)),
                   pl.BlockSpec((1, hk, tt), lambda b, j: (b, 0, j))],
        out_shape=[jax.ShapeDtypeStruct((bsz, hk, n), jnp.int32),
                   jax.ShapeDtypeStruct((bsz, hk, n), F32)],
        compiler_params=_cparams("arbitrary", "arbitrary"),
        name="peer_route",
    )(h, sc, sh, wq.astype(jnp.bfloat16), keys)


def _expert_pitch(rows):
    tiles = -(-rows // V7X_SUBLANES)
    return V7X_SUBLANES * (tiles if tiles % 2 else tiles + 1)


def _peer_expert_kernel(alpha, pitch, idx_ref, gate_ref, h_ref, sc_ref, sh_ref, g2_ref, lng_ref, lnb_ref,
                        uv_hbm, o_ref, uvbuf, xin_scr, f_scr, sems):
    tb, n_sel = gate_ref.shape[1], gate_ref.shape[2]
    h = h_ref[0]
    d = h.shape[-1]
    lanes = uvbuf.shape[-1]
    ru = d // lanes
    rows = 2 * ru
    slot_rows = n_sel * pitch
    per_stage = n_sel // rows
    xin_scr[...] = h * (1.0 + sc_ref[0]) + sh_ref[0]
    step = pl.program_id(0) * pl.num_programs(1) + pl.program_id(1)
    n_steps = pl.num_programs(0) * pl.num_programs(1)

    def issue(t, slot, stage):
        for k in range(stage * per_stage, (stage + 1) * per_stage):
            e = idx_ref[t, k]
            src = uv_hbm.at[pl.ds(pl.multiple_of(e * rows, rows), rows), :]
            dst = uvbuf.at[pl.ds(pl.multiple_of(slot * slot_rows + k * pitch, V7X_SUBLANES), rows), :]
            pltpu.make_async_copy(src, dst, sems.at[slot]).start(priority=k % 2)

    def wait(slot):
        total = n_sel * rows
        pltpu.make_async_copy(uv_hbm.at[pl.ds(0, total), :], uvbuf.at[pl.ds(0, total), :], sems.at[slot]).wait()

    eye = (lax.broadcasted_iota(jnp.int32, (n_sel, n_sel), 0)
           == lax.broadcasted_iota(jnp.int32, (n_sel, n_sel), 1))

    @pl.when(step == 0)
    def _():
        for stage in range(rows):
            issue(0, 0, stage)

    def token(t, carry):
        slot = t % 2
        wait(slot)
        base = slot * slot_rows
        x = xin_scr[pl.ds(t, 1), :]
        g_row = gate_ref[0, pl.ds(t, 1), :]
        g_col = jnp.sum(jnp.where(eye, g_row, 0.0), axis=1, keepdims=True)
        dots = None
        for s in range(ru):
            issue(t + 1, 1 - slot, s)
            u_s = uvbuf[pl.ds(base + s, n_sel, stride=pitch), :]
            part = u_s * x[:, s * lanes:(s + 1) * lanes]
            dots = part if dots is None else dots + part
        coef = g_col * _gelu(jnp.sum(dots, axis=1, keepdims=True))
        f_parts = []
        for s in range(ru):
            issue(t + 1, 1 - slot, ru + s)
            v_s = uvbuf[pl.ds(base + ru + s, n_sel, stride=pitch), :]
            f_parts.append(jnp.sum(coef * v_s, axis=0, keepdims=True))
        f_scr[pl.ds(t, 1), :] = jnp.concatenate(f_parts, axis=1)
        return carry

    lax.fori_loop(0, tb, token, 0)

    @pl.when(step == n_steps - 1)
    def _():
        wait(tb % 2)

    o_ref[0] = _layer_norm(alpha * h + g2_ref[0] * f_scr[...], lng_ref[...], lnb_ref[...])


def _expert_table(u_tab, v_tab):
    n_exp, d = u_tab.shape
    ru = d // V7X_LANES
    uv = jnp.concatenate([u_tab.reshape(n_exp, ru, V7X_LANES), v_tab.reshape(n_exp, ru, V7X_LANES)], axis=1)
    return uv.reshape(n_exp * 2 * ru, V7X_LANES)


def _peer_experts(idx, gate, h, sc, sh, g2, ln_g, ln_b, uv_tab, alpha):
    bsz, n, d = h.shape
    n_sel = idx.shape[-1]
    rows = 2 * d // V7X_LANES
    assert d % V7X_LANES == 0 and n_sel % rows == 0
    pitch = _expert_pitch(rows)
    tb = _token_tile(n, 64)
    assert tb % 2 == 0
    nb = n // tb
    idx_blk = idx.reshape(bsz * nb, tb, n_sel)
    first = idx_blk[:, 0:1, :]
    halo = jnp.concatenate([first[1:], first[-1:]], axis=0)
    halo = jnp.broadcast_to(halo, (bsz * nb, V7X_SUBLANES, n_sel))
    idx_rows = jnp.concatenate([idx_blk, halo], axis=1).reshape(bsz * nb * (tb + V7X_SUBLANES), n_sel)
    tok = lambda b, j: (b, j, 0)
    mod = lambda b, j: (b, 0, 0)
    full = lambda *shape: pl.BlockSpec(shape, lambda b, j: (0,) * len(shape))
    return pl.pallas_call(
        functools.partial(_peer_expert_kernel, alpha, pitch),
        grid=(bsz, nb),
        in_specs=[pl.BlockSpec((tb + V7X_SUBLANES, n_sel), lambda b, j: (b * nb + j, 0),
                               memory_space=pltpu.SMEM),
                  pl.BlockSpec((1, tb, n_sel), tok), pl.BlockSpec((1, tb, d), tok),
                  pl.BlockSpec((1, 1, d), mod), pl.BlockSpec((1, 1, d), mod), pl.BlockSpec((1, 1, d), mod),
                  full(1, d), full(1, d),
                  pl.BlockSpec(memory_space=pl.ANY)],
        out_specs=pl.BlockSpec((1, tb, d), tok),
        out_shape=jax.ShapeDtypeStruct((bsz, n, d), F32),
        scratch_shapes=[pltpu.VMEM((2 * n_sel * pitch, V7X_LANES), F32),
                        pltpu.VMEM((tb, d), F32), pltpu.VMEM((tb, d), F32),
                        pltpu.SemaphoreType.DMA((2,))],
        compiler_params=_cparams("arbitrary", "arbitrary"),
        name="peer_experts",
    )(idx_rows, gate, h, sc, sh, g2, ln_g.reshape(1, d), ln_b.reshape(1, d), uv_tab)


def _peer_block(h, sc2, sh2, g2, wq, keys, uv_tab, ln_g, ln_b, alpha):
    idx, gate = _peer_route(h, sc2, sh2, wq, keys)
    idx = idx.transpose(0, 2, 1)
    gate = gate.transpose(0, 2, 1)
    return _peer_experts(idx, gate, h, sc2, sh2, g2, ln_g, ln_b, uv_tab, alpha)


def kernel(x, c, ctx, c_ctx, ada_w, ada_b, ln_mix_g, ln_mix_b, ln_ffn_g, ln_ffn_b, even_w_in, even_w_out, ssm_lam_re, ssm_lam_im, ssm_log_step, ssm_b_re, ssm_b_im, ssm_c_re, ssm_c_im, ssm_d, glu_w, glu_b, conv_w, conv_b, pool_w, pool_scale, peer_wq, peer_keys, peer_u, peer_v):
    depth, d, _ = ada_w.shape
    bsz, seq, _ = x.shape
    ctx_len = ctx.shape[1]
    ssm_w = ssm_d.shape[-1]
    alpha = (2.0 * depth) ** 0.25

    pad = (-(bsz + 1)) % 8
    c_all = jnp.concatenate([c, c_ctx[None, :], jnp.zeros((pad, d), F32)], axis=0)
    mod = _ada_modulation(c_all, ada_w, ada_b)

    def six(rows):
        return [rows[:, None, k * d:(k + 1) * d] for k in range(6)]

    h_lat, h_ctx = x, ctx
    for layer in range(depth):
        ctx_out = any(j > layer and j % 2 == 0 for j in range(depth))
        even = layer % 2 == 0
        sh1, sc1, g1, sh2, sc2, g2 = six(mod[layer, :bsz])
        need_ctx = ctx_out or even
        if need_ctx:
            csh1, csc1, cg1, csh2, csc2, cg2 = six(jnp.broadcast_to(mod[layer, bsz:bsz + 1], (bsz, 6 * d)))
        lmg, lmb = ln_mix_g[layer], ln_mix_b[layer]
        if even:
            e = layer // 2
            ops = _s5_operators(ssm_lam_re[e], ssm_lam_im[e], ssm_log_step[e], ssm_b_re[e], ssm_b_im[e],
                                ssm_c_re[e], ssm_c_im[e])
            post = functools.partial(_even_post, w_out=even_w_out[e], glu_w=glu_w[e], glu_b=glu_b[e],
                                     conv_w=conv_w[e], conv_b=conv_b[e], ln_g=lmg, ln_b=lmb, alpha=alpha)
            ps_ctx, pc_ctx = _in_projection(h_ctx, csc1, csh1, even_w_in[e], ssm_w)
            y_ctx, finals = _s5_bidir(ps_ctx, ops, ssm_d[e], None)
            ps_lat, pc_lat = _in_projection(h_lat, sc1, sh1, even_w_in[e], ssm_w)
            y_lat, _ = _s5_bidir(ps_lat, ops, ssm_d[e], finals)
            h_lat = post(y_lat, pc_lat, h_lat, g1, seg_len=GRID_W)
            if ctx_out:
                h_ctx = post(y_ctx, pc_ctx, h_ctx, cg1, seg_len=ctx_len)
        else:
            o = layer // 2
            h_lat = _pool_mixer(h_lat, sc1, sh1, g1, pool_w[o], pool_scale[o], lmg, lmb, GRID_W, alpha)
            if ctx_out:
                h_ctx = _pool_mixer(h_ctx, csc1, csh1, cg1, pool_w[o], pool_scale[o], lmg, lmb, ctx_len, alpha)
        uv_tab = _expert_table(peer_u[layer], peer_v[layer])
        peer = functools.partial(_peer_block, wq=peer_wq[layer], keys=peer_keys[layer], uv_tab=uv_tab,
                                 ln_g=ln_ffn_g[layer], ln_b=ln_ffn_b[layer], alpha=alpha)
        h_lat = peer(h_lat, sc2, sh2, g2)
        if ctx_out:
            h_ctx = peer(h_ctx, csc2, csh2, cg2)
    return h_lat
```

```python
import functools
import math

import numpy as np
import jax
import jax.numpy as jnp
from jax import lax
from jax.experimental import pallas as pl
from jax.experimental.pallas import tpu as pltpu

F32 = jnp.float32
HI = lax.Precision.HIGHEST

GRID_W = 64
POOL_WINDOWS = (2, 4, 8, 16)
PEER_TOPK = 16
LN_EPS = 1e-5
STEP_CHUNK = 64
VMEM_LIMIT_V7X = 56 * 1024 * 1024
V7X_SUBLANES, V7X_LANES = 8, 128
NT_DIMS = (((1,), (1,)), ((), ()))


def _cparams(*sem):
    return pltpu.CompilerParams(dimension_semantics=sem, vmem_limit_bytes=VMEM_LIMIT_V7X)


def _dot(a, b):
    return jnp.dot(a, b, preferred_element_type=F32, precision=HI)


def _dot_nt(a, b):
    return lax.dot_general(a, b, NT_DIMS, preferred_element_type=F32, precision=HI)


def _gelu(x):
    return 0.5 * x * (1.0 + lax.erf(x * (1.0 / math.sqrt(2.0))))


def _layer_norm(x, g, b):
    mu = jnp.mean(x, axis=-1, keepdims=True)
    xc = x - mu
    var = jnp.mean(xc * xc, axis=-1, keepdims=True)
    return xc * lax.rsqrt(var + LN_EPS) * g + b


def _token_tile(n, cap):
    t = min(n, cap)
    assert n % t == 0
    return t


def _ada_kernel(c_ref, w_ref, b_ref, o_ref):
    c = c_ref[...]
    s = c * jax.nn.sigmoid(c)
    o_ref[0] = _dot(s, w_ref[0]) + b_ref[0]


def _ada_modulation(c_all, ada_w, ada_b):
    depth, d, e = ada_w.shape
    rows = c_all.shape[0]
    te = e // 6
    return pl.pallas_call(
        _ada_kernel,
        grid=(depth, e // te),
        in_specs=[pl.BlockSpec((rows, d), lambda l, j: (0, 0)),
                  pl.BlockSpec((1, d, te), lambda l, j: (l, 0, j)),
                  pl.BlockSpec((1, 1, te), lambda l, j: (l, 0, j))],
        out_specs=pl.BlockSpec((1, rows, te), lambda l, j: (l, 0, j)),
        out_shape=jax.ShapeDtypeStruct((depth, rows, e), F32),
        compiler_params=_cparams("arbitrary", "arbitrary"),
        name="ada_modulation",
    )(c_all, ada_w, ada_b.reshape(depth, 1, e))


def _inproj_kernel(ssm_w, h_ref, sc_ref, sh_ref, w_ref, ps_ref, pc_ref):
    a = h_ref[0] * (1.0 + sc_ref[0]) + sh_ref[0]
    ps_ref[0] = _dot(a, w_ref[:, :ssm_w])
    pc_ref[0] = _dot(a, w_ref[:, ssm_w:])


def _in_projection(h, sc, sh, w_in, ssm_w):
    bsz, n, d = h.shape
    e = w_in.shape[1]
    tl = _token_tile(n, 512)
    tok = lambda b, j: (b, j, 0)
    mod = lambda b, j: (b, 0, 0)
    return pl.pallas_call(
        functools.partial(_inproj_kernel, ssm_w),
        grid=(bsz, n // tl),
        in_specs=[pl.BlockSpec((1, tl, d), tok),
                  pl.BlockSpec((1, 1, d), mod),
                  pl.BlockSpec((1, 1, d), mod),
                  pl.BlockSpec((d, e), lambda b, j: (0, 0))],
        out_specs=[pl.BlockSpec((1, tl, ssm_w), tok),
                   pl.BlockSpec((1, tl, e - ssm_w), tok)],
        out_shape=[jax.ShapeDtypeStruct((bsz, n, ssm_w), F32),
                   jax.ShapeDtypeStruct((bsz, n, e - ssm_w), F32)],
        compiler_params=_cparams("arbitrary", "arbitrary"),
        name="even_in_projection",
    )(h, sc, sh, w_in)


def _s5_param_kernel(chunk, lam_row_ref, lam_col_ref, ls_ref, bt_re_ref, bt_im_ref,
                     ct_re_ref, ct_im_ref, k_ref, w_re_ref, w_im_ref, g_re_ref, g_im_ref, a_ref):
    backward = pl.program_id(0) == 1
    p_dim = lam_row_ref.shape[-1]
    i_dim = k_ref.shape[-1]
    rows = chunk * i_dim
    step = jnp.exp(ls_ref[0, 0])
    lam_row = lam_row_ref[0, 0]
    lr_row, li_row = lam_row[0:1, :], lam_row[1:2, :]
    lam_col = lam_col_ref[0, 0]
    lr_col, li_col = lam_col[:, 0:1], lam_col[:, 1:2]
    ldr_row, ldi_row = lr_row * step, li_row * step
    ldr_col, ldi_col = lr_col * step, li_col * step

    def power_rows(expo):
        mag = jnp.exp(expo * ldr_row)
        return mag * jnp.cos(expo * ldi_row), mag * jnp.sin(expo * ldi_row)

    mag = jnp.exp(ldr_col)
    ab_re, ab_im = mag * jnp.cos(ldi_col), mag * jnp.sin(ldi_col)
    den = lr_col * lr_col + li_col * li_col
    nr = ab_re - 1.0
    f_re = (nr * lr_col + ab_im * li_col) / den
    f_im = (ab_im * lr_col - nr * li_col) / den
    bt_re, bt_im = bt_re_ref[0, 0], bt_im_ref[0, 0]
    bbt_re = f_re * bt_re - f_im * bt_im
    bbt_im = f_re * bt_im + f_im * bt_re
    bb_re, bb_im = bbt_re[:, :i_dim], bbt_im[:, :i_dim]

    ct_re, ct_im = ct_re_ref[0, 0], ct_im_ref[0, 0]
    tok = (lax.broadcasted_iota(jnp.int32, (rows, 1), 0) // i_dim).astype(F32)

    e_re, e_im = power_rows(tok)
    ck_re = ct_re * e_re - ct_im * e_im
    ck_im = ct_re * e_im + ct_im * e_re
    k_ref[0, 0] = _dot(ck_re, bb_re) - _dot(ck_im, bb_im)

    f_exp = jnp.where(backward, float(chunk) - tok, tok + 1.0)
    e_re, e_im = power_rows(f_exp)
    g_re_ref[0, 0] = ct_re * e_re - ct_im * e_im
    g_im_ref[0, 0] = ct_re * e_im + ct_im * e_re

    tok_l = (lax.broadcasted_iota(jnp.int32, (1, rows), 1) // i_dim).astype(F32)
    e_exp = jnp.where(backward, tok_l, float(chunk - 1) - tok_l)
    magw = jnp.exp(ldr_col * e_exp)
    ew_re, ew_im = magw * jnp.cos(ldi_col * e_exp), magw * jnp.sin(ldi_col * e_exp)
    w_re_ref[0, 0] = ew_re * bbt_re - ew_im * bbt_im
    w_im_ref[0, 0] = ew_re * bbt_im + ew_im * bbt_re

    full = jnp.full((1, 1), float(chunk), F32)
    a_re, a_im = power_rows(full)
    a_ref[0, 0] = jnp.concatenate([a_re, a_im], axis=0)


def _s5_operators(lam_re, lam_im, log_step, b_re, b_im, c_re, c_im):
    _, g, p = lam_re.shape
    i = b_re.shape[-1]
    t = STEP_CHUNK
    rows = t * i
    lam_row = jnp.stack([lam_re, lam_im], axis=2)
    lam_col = jnp.stack([lam_re, lam_im], axis=3)
    ls = log_step.reshape(2, g, 1, 1)
    bt_re = jnp.tile(b_re, (1, 1, 1, t))
    bt_im = jnp.tile(b_im, (1, 1, 1, t))
    ct_re = jnp.tile(c_re, (1, 1, t, 1))
    ct_im = jnp.tile(c_im, (1, 1, t, 1))
    blk = lambda *shape: pl.BlockSpec((1, 1) + shape, lambda dr, gi: (dr, gi, 0, 0))
    k, w_re, w_im, g_re, g_im, a = pl.pallas_call(
        functools.partial(_s5_param_kernel, t),
        grid=(2, g),
        in_specs=[blk(2, p), blk(p, 2), blk(1, 1), blk(p, rows), blk(p, rows),
                  blk(rows, p), blk(rows, p)],
        out_specs=[blk(rows, i), blk(p, rows), blk(p, rows), blk(rows, p), blk(rows, p), blk(2, p)],
        out_shape=[jax.ShapeDtypeStruct((2, g, rows, i), F32),
                   jax.ShapeDtypeStruct((2, g, p, rows), F32),
                   jax.ShapeDtypeStruct((2, g, p, rows), F32),
                   jax.ShapeDtypeStruct((2, g, rows, p), F32),
                   jax.ShapeDtypeStruct((2, g, rows, p), F32),
                   jax.ShapeDtypeStruct((2, g, 2, p), F32)],
        compiler_params=_cparams("arbitrary", "arbitrary"),
        name="s5_operators",
    )(lam_row, lam_col, ls, bt_re, bt_im, ct_re, ct_im)

    k5 = k.reshape(2, g, t, i, i)
    s_idx = np.arange(t)[:, None]
    t_idx = np.arange(t)[None, :]

    def toeplitz(kd, lag):
        m = kd[:, np.clip(lag, 0, t - 1)]
        m = jnp.where((lag >= 0)[None, :, :, None, None], m, 0.0)
        return m.transpose(0, 1, 4, 2, 3).reshape(g, rows, rows)

    m = jnp.stack([toeplitz(k5[0], t_idx - s_idx), toeplitz(k5[1], s_idx - t_idx)])
    return m, w_re, w_im, g_re, g_im, a


def _s5_scan_kernel(bsz, u_ref, d_ref, m_ref, w_re_ref, w_im_ref, g_re_ref, g_im_ref, a_ref,
                    h0_re_ref, h0_im_ref, y_ref, hf_re_ref, hf_im_ref,
                    z_re_scr, z_im_scr, hp_re_scr, hp_im_scr):
    backward = pl.program_id(1) == 1
    u = u_ref[0]
    n_chunks = u.shape[0] // bsz
    z_re_scr[...] = _dot_nt(u, w_re_ref[0, 0])
    z_im_scr[...] = _dot_nt(u, w_im_ref[0, 0])
    a = a_ref[0, 0]
    a_re, a_im = a[0:1, :], a[1:2, :]

    def chunk_step(k, carry):
        h_re, h_im = carry
        c = jnp.where(backward, n_chunks - 1 - k, k)
        rows = pl.ds(pl.multiple_of(c * bsz, bsz), bsz)
        hp_re_scr[rows, :] = h_re
        hp_im_scr[rows, :] = h_im
        n_re = a_re * h_re - a_im * h_im + z_re_scr[rows, :]
        n_im = a_re * h_im + a_im * h_re + z_im_scr[rows, :]
        return n_re, n_im

    h_re, h_im = lax.fori_loop(0, n_chunks, chunk_step, (h0_re_ref[0, 0], h0_im_ref[0, 0]))
    hf_re_ref[0, 0] = h_re
    hf_im_ref[0, 0] = h_im

    y = (_dot(u, m_ref[0, 0]) + _dot_nt(hp_re_scr[...], g_re_ref[0, 0])
         - _dot_nt(hp_im_scr[...], g_im_ref[0, 0]))

    @pl.when(jnp.logical_not(backward))
    def _():
        y_ref[0] = d_ref[0] * u + y

    @pl.when(backward)
    def _():
        y_ref[0] = y_ref[0] + y


def _s5_bidir(u, ops, d_skip, h0):
    m, w_re, w_im, g_re, g_im, a = ops
    bsz, n, width = u.shape
    _, g, p, rows = w_re.shape
    t = STEP_CHUNK
    i = rows // t
    nc = n // t
    r = nc * bsz
    ug = u.reshape(bsz, nc, t, g, i).transpose(3, 1, 0, 2, 4).reshape(g, r, rows)
    dg = jnp.tile(d_skip.reshape(g, 1, i), (1, 1, t))
    if h0 is None:
        h0 = (jnp.zeros((2, g, bsz, p), F32),) * 2
    per_g = lambda *shape: pl.BlockSpec((1,) + shape, lambda gi, dr: (gi, 0, 0))
    per_dg = lambda *shape: pl.BlockSpec((1, 1) + shape, lambda gi, dr: (dr, gi, 0, 0))
    y, hf_re, hf_im = pl.pallas_call(
        functools.partial(_s5_scan_kernel, bsz),
        grid=(g, 2),
        in_specs=[per_g(r, rows), per_g(1, rows), per_dg(rows, rows), per_dg(p, rows), per_dg(p, rows),
                  per_dg(rows, p), per_dg(rows, p), per_dg(2, p), per_dg(bsz, p), per_dg(bsz, p)],
        out_specs=[per_g(r, rows), per_dg(bsz, p), per_dg(bsz, p)],
        out_shape=[jax.ShapeDtypeStruct((g, r, rows), F32),
                   jax.ShapeDtypeStruct((2, g, bsz, p), F32),
                   jax.ShapeDtypeStruct((2, g, bsz, p), F32)],
        scratch_shapes=[pltpu.VMEM((r, p), F32)] * 4,
        compiler_params=_cparams("arbitrary", "arbitrary"),
        name="s5_scan",
    )(ug, dg, m, w_re, w_im, g_re, g_im, a, h0[0], h0[1])
    y = y.reshape(g, nc, bsz, t, i).transpose(2, 1, 3, 0, 4).reshape(bsz, n, width)
    return y, (hf_re, hf_im)


def _even_post_kernel(seg_len, alpha, ys_ref, pc_ref, h_ref, g1_ref, wout_ref, gluw_ref, glub_ref,
                      cw_ref, cb_ref, lng_ref, lnb_ref, o_ref):
    ys = ys_ref[0]
    tl, sw = ys.shape
    g = _gelu(ys)
    ya = g * jax.nn.sigmoid(_dot(g, gluw_ref[...]) + glub_ref[...])
    cwid = pc_ref.shape[-1] // 3
    z = pc_ref[0, :, cwid:2 * cwid] * pc_ref[0, :, 2 * cwid:]
    pos = lax.broadcasted_iota(jnp.int32, (tl, 1), 0) % seg_len
    z_prev = jnp.where(pos == 0, 0.0, pltpu.roll(z, 1, axis=0))
    z_next = jnp.where(pos == seg_len - 1, 0.0, pltpu.roll(z, tl - 1, axis=0))
    cw = cw_ref[...]
    conv = cw[0:1, :] * z_prev + cw[1:2, :] * z + cw[2:3, :] * z_next + cb_ref[...]
    yb = pc_ref[0, :, :cwid] * conv
    out = _dot(ya, wout_ref[:sw, :]) + _dot(yb, wout_ref[sw:, :])
    h = h_ref[0]
    o_ref[0] = _layer_norm(alpha * h + g1_ref[0] * out, lng_ref[...], lnb_ref[...])


def _even_post(y_ssm, p_conv, h, g1, w_out, glu_w, glu_b, conv_w, conv_b, ln_g, ln_b, seg_len, alpha):
    bsz, n, d = h.shape
    sw = y_ssm.shape[-1]
    cw3 = p_conv.shape[-1]
    cwid = cw3 // 3
    tl = _token_tile(n, 512)
    assert tl % seg_len == 0
    tok = lambda b, j: (b, j, 0)
    mod = lambda b, j: (b, 0, 0)
    full = lambda *shape: pl.BlockSpec(shape, lambda b, j: (0,) * len(shape))
    return pl.pallas_call(
        functools.partial(_even_post_kernel, seg_len, alpha),
        grid=(bsz, n // tl),
        in_specs=[pl.BlockSpec((1, tl, sw), tok), pl.BlockSpec((1, tl, cw3), tok),
                  pl.BlockSpec((1, tl, d), tok), pl.BlockSpec((1, 1, d), mod),
                  full(sw + cwid, d), full(sw, sw), full(1, sw), full(3, cwid), full(1, cwid),
                  full(1, d), full(1, d)],
        out_specs=pl.BlockSpec((1, tl, d), tok),
        out_shape=jax.ShapeDtypeStruct((bsz, n, d), F32),
        compiler_params=_cparams("arbitrary", "arbitrary"),
        name="even_post",
    )(y_ssm, p_conv, h, g1, w_out, glu_w, glu_b.reshape(1, sw), conv_w, conv_b.reshape(1, cwid),
      ln_g.reshape(1, d), ln_b.reshape(1, d))


def _pool_matrices(seg_len, tile):
    pos = np.arange(tile) % seg_len
    seg = np.arange(tile) // seg_len
    mats = []
    for w in POOL_WINDOWS:
        lo = np.clip(pos - w // 2, 0, seg_len - 1)
        hi = np.clip(pos + w // 2 - 1, 0, seg_len - 1)
        cnt = (hi - lo + 1).astype(np.float64)
        inside = ((seg[:, None] == seg[None, :]) & (pos[None, :] >= lo[:, None])
                  & (pos[None, :] <= hi[:, None]))
        mats.append(inside / cnt[:, None] - np.eye(tile))
    return np.stack(mats).astype(np.float32)


def _pool_kernel(alpha, h_ref, sc_ref, sh_ref, g1_ref, pm_ref, pw_ref, ps_ref, lng_ref, lnb_ref, o_ref):
    h = h_ref[0]
    a = h * (1.0 + sc_ref[0]) + sh_ref[0]
    n_win, pg, _ = pw_ref.shape
    ys = []
    for gi in range(n_win):
        z = _dot(pm_ref[gi], a[:, gi * pg:(gi + 1) * pg])
        ys.append(_dot(z, pw_ref[gi]))
    y = jnp.concatenate(ys, axis=-1) * ps_ref[...]
    o_ref[0] = _layer_norm(alpha * h + g1_ref[0] * y, lng_ref[...], lnb_ref[...])


def _pool_mixer(h, sc, sh, g1, pool_w, pool_scale, ln_g, ln_b, seg_len, alpha):
    bsz, n, d = h.shape
    tl = _token_tile(n, 256)
    assert tl % seg_len == 0
    pm = jnp.asarray(_pool_matrices(seg_len, tl))
    n_win, pg, _ = pool_w.shape
    tok = lambda b, j: (b, j, 0)
    mod = lambda b, j: (b, 0, 0)
    full = lambda *shape: pl.BlockSpec(shape, lambda b, j: (0,) * len(shape))
    return pl.pallas_call(
        functools.partial(_pool_kernel, alpha),
        grid=(bsz, n // tl),
        in_specs=[pl.BlockSpec((1, tl, d), tok), pl.BlockSpec((1, 1, d), mod), pl.BlockSpec((1, 1, d), mod),
                  pl.BlockSpec((1, 1, d), mod), full(n_win, tl, tl), full(n_win, pg, pg), full(1, d),
                  full(1, d), full(1, d)],
        out_specs=pl.BlockSpec((1, tl, d), tok),
        out_shape=jax.ShapeDtypeStruct((bsz, n, d), F32),
        compiler_params=_cparams("arbitrary", "arbitrary"),
        name="pool_mixer",
    )(h, sc, sh, g1, pm, pool_w, pool_scale.reshape(1, d), ln_g.reshape(1, d), ln_b.reshape(1, d))


def _top_rows(s, n_top):
    n = s.shape[0]
    rid = lax.broadcasted_iota(jnp.int32, s.shape, 0).astype(F32)
    vals, rows = [], []
    for _ in range(n_top):
        m = jnp.max(s, axis=0, keepdims=True)
        sel = jnp.min(jnp.where(s == m, rid, float(n)), axis=0, keepdims=True)
        vals.append(m)
        rows.append(sel)
        s = jnp.where(rid == sel, -jnp.inf, s)
    return jnp.concatenate(vals, axis=0), jnp.concatenate(rows, axis=0)


def _candidate_slabs(k_top):
    slabs = []
    for a in range(k_top):
        n_valid = k_top // (a + 1)
        if n_valid == 1:
            slabs.append((a, k_top, 1, 1))
            break
        slabs.append((a, a + 1, -(-n_valid // 8) * 8, n_valid))
    return slabs


def _peer_route_kernel(n_heads, h_ref, sc_ref, sh_ref, wq_ref, keys_ref, idx_ref, gate_ref):
    xin = (h_ref[0] * (1.0 + sc_ref[0]) + sh_ref[0]).astype(jnp.bfloat16)
    n_keys, half = keys_ref.shape[1], keys_ref.shape[2]
    k_top = PEER_TOPK
    slabs = _candidate_slabs(k_top)
    for head in range(n_heads):
        tops = []
        for side in range(2):
            col = (head * 2 + side) * half
            q = jnp.dot(xin, wq_ref[:, col:col + half], preferred_element_type=F32)
            s = _dot_nt(keys_ref[side], q)
            tops.append(_top_rows(s, k_top))
        (v0, i0), (v1, i1) = tops
        cs, ci = [], []
        for a_lo, a_hi, n_b, n_valid in slabs:
            b_hi = 1 if n_b == 1 else n_b
            sl = v0[a_lo:a_hi] + v1[0:b_hi]
            if n_valid < n_b:
                sl = jnp.where(lax.broadcasted_iota(jnp.int32, sl.shape, 0) < n_valid, sl, -jnp.inf)
            cs.append(sl)
            ci.append(i0[a_lo:a_hi] * float(n_keys) + i1[0:b_hi])
        cand_s = jnp.concatenate(cs, axis=0)
        cand_i = jnp.concatenate(ci, axis=0)
        n_cand = cand_s.shape[0]
        rid = lax.broadcasted_iota(jnp.int32, cand_s.shape, 0).astype(F32)
        best_s, best_i = [], []
        s = cand_s
        for _ in range(k_top):
            m = jnp.max(s, axis=0, keepdims=True)
            sel = jnp.min(jnp.where(s == m, rid, float(n_cand)), axis=0, keepdims=True)
            hit = rid == sel
            best_s.append(m)
            best_i.append(jnp.sum(jnp.where(hit, cand_i, 0.0), axis=0, keepdims=True))
            s = jnp.where(hit, -jnp.inf, s)
        bs = jnp.concatenate(best_s, axis=0)
        e = jnp.exp(bs - bs[0:1, :])
        gate = e / jnp.sum(e, axis=0, keepdims=True)
        idx_ref[0, head * k_top:(head + 1) * k_top, :] = jnp.concatenate(best_i, axis=0).astype(jnp.int32)
        gate_ref[0, head * k_top:(head + 1) * k_top, :] = gate


def _peer_route(h, sc, sh, wq, keys):
    bsz, n, d = h.shape
    _, n_keys, half = keys.shape
    n_heads = wq.shape[1] // (2 * half)
    tt = _token_tile(n, 256)
    hk = n_heads * PEER_TOPK
    mod = lambda b, j: (b, 0, 0)
    return pl.pallas_call(
        functools.partial(_peer_route_kernel, n_heads),
        grid=(bsz, n // tt),
        in_specs=[pl.BlockSpec((1, tt, d), lambda b, j: (b, j, 0)),
                  pl.BlockSpec((1, 1, d), mod), pl.BlockSpec((1, 1, d), mod),
                  pl.BlockSpec(wq.shape, lambda b, j: (0, 0)),
                  pl.BlockSpec(keys.shape, lambda b, j: (0, 0, 0))],
        out_specs=[pl.BlockSpec((1, hk, tt), lambda b, j: (b, 0, j)),
                   pl.BlockSpec((1, hk, tt), lambda b, j: (b, 0, j))],
        out_shape=[jax.ShapeDtypeStruct((bsz, hk, n), jnp.int32),
                   jax.ShapeDtypeStruct((bsz, hk, n), F32)],
        compiler_params=_cparams("arbitrary", "arbitrary"),
        name="peer_route",
    )(h, sc, sh, wq.astype(jnp.bfloat16), keys)


EXPERT_SLOTS = 4
PREFETCH_TOKENS = EXPERT_SLOTS - 1
assert PREFETCH_TOKENS <= V7X_SUBLANES


def _expert_pitch(rows):
    tiles = -(-rows // V7X_SUBLANES)
    return V7X_SUBLANES * (tiles if tiles % 2 else tiles + 1)


def _peer_expert_kernel(alpha, pitch, idx_ref, gate_ref, h_ref, sc_ref, sh_ref, g2_ref, lng_ref, lnb_ref,
                        uv_hbm, o_ref, uvbuf, xin_scr, f_scr, sems):
    tb, n_sel = gate_ref.shape[1], gate_ref.shape[2]
    h = h_ref[0]
    d = h.shape[-1]
    lanes = uvbuf.shape[-1]
    ru = d // lanes
    rows = 2 * ru
    slot_rows = n_sel * pitch
    per_stage = n_sel // rows
    xin_scr[...] = h * (1.0 + sc_ref[0]) + sh_ref[0]
    step = pl.program_id(0) * pl.num_programs(1) + pl.program_id(1)
    n_steps = pl.num_programs(0) * pl.num_programs(1)

    def issue(t, slot, stage):
        for k in range(stage * per_stage, (stage + 1) * per_stage):
            e = idx_ref[t, k]
            src = uv_hbm.at[pl.ds(pl.multiple_of(e * rows, rows), rows), :]
            dst = uvbuf.at[pl.ds(pl.multiple_of(slot * slot_rows + k * pitch, V7X_SUBLANES), rows), :]
            pltpu.make_async_copy(src, dst, sems.at[slot]).start(priority=k % 2)

    def wait(slot):
        total = n_sel * rows
        pltpu.make_async_copy(uv_hbm.at[pl.ds(0, total), :], uvbuf.at[pl.ds(0, total), :], sems.at[slot]).wait()

    eye = (lax.broadcasted_iota(jnp.int32, (n_sel, n_sel), 0)
           == lax.broadcasted_iota(jnp.int32, (n_sel, n_sel), 1))

    @pl.when(step == 0)
    def _():
        for t0 in range(PREFETCH_TOKENS):
            for stage in range(rows):
                issue(t0, t0, stage)

    def token(t, carry):
        slot = t % EXPERT_SLOTS
        ahead = (t + PREFETCH_TOKENS) % EXPERT_SLOTS
        wait(slot)
        base = slot * slot_rows
        x = xin_scr[pl.ds(t, 1), :]
        g_row = gate_ref[0, pl.ds(t, 1), :]
        g_col = jnp.sum(jnp.where(eye, g_row, 0.0), axis=1, keepdims=True)
        dots = None
        for s in range(ru):
            issue(t + PREFETCH_TOKENS, ahead, s)
            u_s = uvbuf[pl.ds(base + s, n_sel, stride=pitch), :]
            part = u_s * x[:, s * lanes:(s + 1) * lanes]
            dots = part if dots is None else dots + part
        coef = g_col * _gelu(jnp.sum(dots, axis=1, keepdims=True))
        f_parts = []
        for s in range(ru):
            issue(t + PREFETCH_TOKENS, ahead, ru + s)
            v_s = uvbuf[pl.ds(base + ru + s, n_sel, stride=pitch), :]
            f_parts.append(jnp.sum(coef * v_s, axis=0, keepdims=True))
        f_scr[pl.ds(t, 1), :] = jnp.concatenate(f_parts, axis=1)
        return carry

    lax.fori_loop(0, tb, token, 0)

    @pl.when(step == n_steps - 1)
    def _():
        for t0 in range(PREFETCH_TOKENS):
            wait((tb + t0) % EXPERT_SLOTS)

    o_ref[0] = _layer_norm(alpha * h + g2_ref[0] * f_scr[...], lng_ref[...], lnb_ref[...])


def _expert_table(u_tab, v_tab):
    n_exp, d = u_tab.shape
    ru = d // V7X_LANES
    uv = jnp.concatenate([u_tab.reshape(n_exp, ru, V7X_LANES), v_tab.reshape(n_exp, ru, V7X_LANES)], axis=1)
    return uv.reshape(n_exp * 2 * ru, V7X_LANES)


def _peer_experts(idx, gate, h, sc, sh, g2, ln_g, ln_b, uv_tab, alpha):
    bsz, n, d = h.shape
    n_sel = idx.shape[-1]
    rows = 2 * d // V7X_LANES
    assert d % V7X_LANES == 0 and n_sel % rows == 0
    pitch = _expert_pitch(rows)
    tb = _token_tile(n, 64)
    assert tb % EXPERT_SLOTS == 0
    nb = n // tb
    idx_blk = idx.reshape(bsz * nb, tb, n_sel)
    first = idx_blk[:, :V7X_SUBLANES, :]
    halo = jnp.concatenate([first[1:], first[-1:]], axis=0)
    idx_rows = jnp.concatenate([idx_blk, halo], axis=1).reshape(bsz * nb * (tb + V7X_SUBLANES), n_sel)
    tok = lambda b, j: (b, j, 0)
    mod = lambda b, j: (b, 0, 0)
    full = lambda *shape: pl.BlockSpec(shape, lambda b, j: (0,) * len(shape))
    return pl.pallas_call(
        functools.partial(_peer_expert_kernel, alpha, pitch),
        grid=(bsz, nb),
        in_specs=[pl.BlockSpec((tb + V7X_SUBLANES, n_sel), lambda b, j: (b * nb + j, 0),
                               memory_space=pltpu.SMEM),
                  pl.BlockSpec((1, tb, n_sel), tok), pl.BlockSpec((1, tb, d), tok),
                  pl.BlockSpec((1, 1, d), mod), pl.BlockSpec((1, 1, d), mod), pl.BlockSpec((1, 1, d), mod),
                  full(1, d), full(1, d),
                  pl.BlockSpec(memory_space=pl.ANY)],
        out_specs=pl.BlockSpec((1, tb, d), tok),
        out_shape=jax.ShapeDtypeStruct((bsz, n, d), F32),
        scratch_shapes=[pltpu.VMEM((EXPERT_SLOTS * n_sel * pitch, V7X_LANES), F32),
                        pltpu.VMEM((tb, d), F32), pltpu.VMEM((tb, d), F32),
                        pltpu.SemaphoreType.DMA((EXPERT_SLOTS,))],
        compiler_params=_cparams("arbitrary", "arbitrary"),
        name="peer_experts",
    )(idx_rows, gate, h, sc, sh, g2, ln_g.reshape(1, d), ln_b.reshape(1, d), uv_tab)


def _peer_block(h, sc2, sh2, g2, wq, keys, uv_tab, ln_g, ln_b, alpha):
    idx, gate = _peer_route(h, sc2, sh2, wq, keys)
    idx = idx.transpose(0, 2, 1)
    gate = gate.transpose(0, 2, 1)
    return _peer_experts(idx, gate, h, sc2, sh2, g2, ln_g, ln_b, uv_tab, alpha)


def kernel(x, c, ctx, c_ctx, ada_w, ada_b, ln_mix_g, ln_mix_b, ln_ffn_g, ln_ffn_b, even_w_in, even_w_out, ssm_lam_re, ssm_lam_im, ssm_log_step, ssm_b_re, ssm_b_im, ssm_c_re, ssm_c_im, ssm_d, glu_w, glu_b, conv_w, conv_b, pool_w, pool_scale, peer_wq, peer_keys, peer_u, peer_v):
    depth, d, _ = ada_w.shape
    bsz, seq, _ = x.shape
    ctx_len = ctx.shape[1]
    ssm_w = ssm_d.shape[-1]
    alpha = (2.0 * depth) ** 0.25

    pad = (-(bsz + 1)) % 8
    c_all = jnp.concatenate([c, c_ctx[None, :], jnp.zeros((pad, d), F32)], axis=0)
    mod = _ada_modulation(c_all, ada_w, ada_b)

    def six(rows):
        return [rows[:, None, k * d:(k + 1) * d] for k in range(6)]

    h_lat, h_ctx = x, ctx
    for layer in range(depth):
        ctx_out = any(j > layer and j % 2 == 0 for j in range(depth))
        even = layer % 2 == 0
        sh1, sc1, g1, sh2, sc2, g2 = six(mod[layer, :bsz])
        need_ctx = ctx_out or even
        if need_ctx:
            csh1, csc1, cg1, csh2, csc2, cg2 = six(jnp.broadcast_to(mod[layer, bsz:bsz + 1], (bsz, 6 * d)))
        lmg, lmb = ln_mix_g[layer], ln_mix_b[layer]
        if even:
            e = layer // 2
            ops = _s5_operators(ssm_lam_re[e], ssm_lam_im[e], ssm_log_step[e], ssm_b_re[e], ssm_b_im[e],
                                ssm_c_re[e], ssm_c_im[e])
            post = functools.partial(_even_post, w_out=even_w_out[e], glu_w=glu_w[e], glu_b=glu_b[e],
                                     conv_w=conv_w[e], conv_b=conv_b[e], ln_g=lmg, ln_b=lmb, alpha=alpha)
            ps_ctx, pc_ctx = _in_projection(h_ctx, csc1, csh1, even_w_in[e], ssm_w)
            y_ctx, finals = _s5_bidir(ps_ctx, ops, ssm_d[e], None)
            ps_lat, pc_lat = _in_projection(h_lat, sc1, sh1, even_w_in[e], ssm_w)
            y_lat, _ = _s5_bidir(ps_lat, ops, ssm_d[e], finals)
            h_lat = post(y_lat, pc_lat, h_lat, g1, seg_len=GRID_W)
            if ctx_out:
                h_ctx = post(y_ctx, pc_ctx, h_ctx, cg1, seg_len=ctx_len)
        else:
            o = layer // 2
            h_lat = _pool_mixer(h_lat, sc1, sh1, g1, pool_w[o], pool_scale[o], lmg, lmb, GRID_W, alpha)
            if ctx_out:
                h_ctx = _pool_mixer(h_ctx, csc1, csh1, cg1, pool_w[o], pool_scale[o], lmg, lmb, ctx_len, alpha)
        uv_tab = _expert_table(peer_u[layer], peer_v[layer])
        peer = functools.partial(_peer_block, wq=peer_wq[layer], keys=peer_keys[layer], uv_tab=uv_tab,
                                 ln_g=ln_ffn_g[layer], ln_b=ln_ffn_b[layer], alpha=alpha)
        h_lat = peer(h_lat, sc2, sh2, g2)
        if ctx_out:
            h_ctx = peer(h_ctx, csc2, csh2, cg2)
    return h_lat
```

```python
import functools
import math

import numpy as np
import jax
import jax.numpy as jnp
from jax import lax
from jax.experimental import pallas as pl
from jax.experimental.pallas import tpu as pltpu

F32 = jnp.float32
HI = lax.Precision.HIGHEST

GRID_W = 64
POOL_WINDOWS = (2, 4, 8, 16)
PEER_TOPK = 16
LN_EPS = 1e-5
STEP_CHUNK = 64
VMEM_LIMIT_V7X = 56 * 1024 * 1024
V7X_SUBLANES, V7X_LANES = 8, 128
NT_DIMS = (((1,), (1,)), ((), ()))


def _cparams(*sem):
    return pltpu.CompilerParams(dimension_semantics=sem, vmem_limit_bytes=VMEM_LIMIT_V7X)


def _dot(a, b):
    return jnp.dot(a, b, preferred_element_type=F32, precision=HI)


def _dot_nt(a, b):
    return lax.dot_general(a, b, NT_DIMS, preferred_element_type=F32, precision=HI)


def _gelu(x):
    return 0.5 * x * (1.0 + lax.erf(x * (1.0 / math.sqrt(2.0))))


def _layer_norm(x, g, b):
    mu = jnp.mean(x, axis=-1, keepdims=True)
    xc = x - mu
    var = jnp.mean(xc * xc, axis=-1, keepdims=True)
    return xc * lax.rsqrt(var + LN_EPS) * g + b


def _token_tile(n, cap):
    t = min(n, cap)
    assert n % t == 0
    return t


def _ada_kernel(c_ref, w_ref, b_ref, o_ref):
    c = c_ref[...]
    s = c * jax.nn.sigmoid(c)
    o_ref[0] = _dot(s, w_ref[0]) + b_ref[0]


def _ada_modulation(c_all, ada_w, ada_b):
    depth, d, e = ada_w.shape
    rows = c_all.shape[0]
    te = e // 6
    return pl.pallas_call(
        _ada_kernel,
        grid=(depth, e // te),
        in_specs=[pl.BlockSpec((rows, d), lambda l, j: (0, 0)),
                  pl.BlockSpec((1, d, te), lambda l, j: (l, 0, j)),
                  pl.BlockSpec((1, 1, te), lambda l, j: (l, 0, j))],
        out_specs=pl.BlockSpec((1, rows, te), lambda l, j: (l, 0, j)),
        out_shape=jax.ShapeDtypeStruct((depth, rows, e), F32),
        compiler_params=_cparams("arbitrary", "arbitrary"),
        name="ada_modulation",
    )(c_all, ada_w, ada_b.reshape(depth, 1, e))


def _inproj_kernel(ssm_w, h_ref, sc_ref, sh_ref, w_ref, ps_ref, pc_ref):
    a = h_ref[0] * (1.0 + sc_ref[0]) + sh_ref[0]
    ps_ref[0] = _dot(a, w_ref[:, :ssm_w])
    pc_ref[0] = _dot(a, w_ref[:, ssm_w:])


def _in_projection(h, sc, sh, w_in, ssm_w):
    bsz, n, d = h.shape
    e = w_in.shape[1]
    tl = _token_tile(n, 512)
    tok = lambda b, j: (b, j, 0)
    mod = lambda b, j: (b, 0, 0)
    return pl.pallas_call(
        functools.partial(_inproj_kernel, ssm_w),
        grid=(bsz, n // tl),
        in_specs=[pl.BlockSpec((1, tl, d), tok),
                  pl.BlockSpec((1, 1, d), mod),
                  pl.BlockSpec((1, 1, d), mod),
                  pl.BlockSpec((d, e), lambda b, j: (0, 0))],
        out_specs=[pl.BlockSpec((1, tl, ssm_w), tok),
                   pl.BlockSpec((1, tl, e - ssm_w), tok)],
        out_shape=[jax.ShapeDtypeStruct((bsz, n, ssm_w), F32),
                   jax.ShapeDtypeStruct((bsz, n, e - ssm_w), F32)],
        compiler_params=_cparams("arbitrary", "arbitrary"),
        name="even_in_projection",
    )(h, sc, sh, w_in)


def _s5_param_kernel(chunk, lam_row_ref, lam_col_ref, ls_ref, bt_re_ref, bt_im_ref,
                     ct_re_ref, ct_im_ref, k_ref, w_re_ref, w_im_ref, g_re_ref, g_im_ref, a_ref):
    backward = pl.program_id(0) == 1
    p_dim = lam_row_ref.shape[-1]
    i_dim = k_ref.shape[-1]
    rows = chunk * i_dim
    step = jnp.exp(ls_ref[0, 0])
    lam_row = lam_row_ref[0, 0]
    lr_row, li_row = lam_row[0:1, :], lam_row[1:2, :]
    lam_col = lam_col_ref[0, 0]
    lr_col, li_col = lam_col[:, 0:1], lam_col[:, 1:2]
    ldr_row, ldi_row = lr_row * step, li_row * step
    ldr_col, ldi_col = lr_col * step, li_col * step

    def power_rows(expo):
        mag = jnp.exp(expo * ldr_row)
        return mag * jnp.cos(expo * ldi_row), mag * jnp.sin(expo * ldi_row)

    mag = jnp.exp(ldr_col)
    ab_re, ab_im = mag * jnp.cos(ldi_col), mag * jnp.sin(ldi_col)
    den = lr_col * lr_col + li_col * li_col
    nr = ab_re - 1.0
    f_re = (nr * lr_col + ab_im * li_col) / den
    f_im = (ab_im * lr_col - nr * li_col) / den
    bt_re, bt_im = bt_re_ref[0, 0], bt_im_ref[0, 0]
    bbt_re = f_re * bt_re - f_im * bt_im
    bbt_im = f_re * bt_im + f_im * bt_re
    bb_re, bb_im = bbt_re[:, :i_dim], bbt_im[:, :i_dim]

    ct_re, ct_im = ct_re_ref[0, 0], ct_im_ref[0, 0]
    tok = (lax.broadcasted_iota(jnp.int32, (rows, 1), 0) // i_dim).astype(F32)

    e_re, e_im = power_rows(tok)
    ck_re = ct_re * e_re - ct_im * e_im
    ck_im = ct_re * e_im + ct_im * e_re
    k_ref[0, 0] = _dot(ck_re, bb_re) - _dot(ck_im, bb_im)

    f_exp = jnp.where(backward, float(chunk) - tok, tok + 1.0)
    e_re, e_im = power_rows(f_exp)
    g_re_ref[0, 0] = ct_re * e_re - ct_im * e_im
    g_im_ref[0, 0] = ct_re * e_im + ct_im * e_re

    tok_l = (lax.broadcasted_iota(jnp.int32, (1, rows), 1) // i_dim).astype(F32)
    e_exp = jnp.where(backward, tok_l, float(chunk - 1) - tok_l)
    magw = jnp.exp(ldr_col * e_exp)
    ew_re, ew_im = magw * jnp.cos(ldi_col * e_exp), magw * jnp.sin(ldi_col * e_exp)
    w_re_ref[0, 0] = ew_re * bbt_re - ew_im * bbt_im
    w_im_ref[0, 0] = ew_re * bbt_im + ew_im * bbt_re

    full = jnp.full((1, 1), float(chunk), F32)
    a_re, a_im = power_rows(full)
    a_ref[0, 0] = jnp.concatenate([a_re, a_im], axis=0)


def _s5_operators(lam_re, lam_im, log_step, b_re, b_im, c_re, c_im):
    _, g, p = lam_re.shape
    i = b_re.shape[-1]
    t = STEP_CHUNK
    rows = t * i
    lam_row = jnp.stack([lam_re, lam_im], axis=2)
    lam_col = jnp.stack([lam_re, lam_im], axis=3)
    ls = log_step.reshape(2, g, 1, 1)
    bt_re = jnp.tile(b_re, (1, 1, 1, t))
    bt_im = jnp.tile(b_im, (1, 1, 1, t))
    ct_re = jnp.tile(c_re, (1, 1, t, 1))
    ct_im = jnp.tile(c_im, (1, 1, t, 1))
    blk = lambda *shape: pl.BlockSpec((1, 1) + shape, lambda dr, gi: (dr, gi, 0, 0))
    k, w_re, w_im, g_re, g_im, a = pl.pallas_call(
        functools.partial(_s5_param_kernel, t),
        grid=(2, g),
        in_specs=[blk(2, p), blk(p, 2), blk(1, 1), blk(p, rows), blk(p, rows),
                  blk(rows, p), blk(rows, p)],
        out_specs=[blk(rows, i), blk(p, rows), blk(p, rows), blk(rows, p), blk(rows, p), blk(2, p)],
        out_shape=[jax.ShapeDtypeStruct((2, g, rows, i), F32),
                   jax.ShapeDtypeStruct((2, g, p, rows), F32),
                   jax.ShapeDtypeStruct((2, g, p, rows), F32),
                   jax.ShapeDtypeStruct((2, g, rows, p), F32),
                   jax.ShapeDtypeStruct((2, g, rows, p), F32),
                   jax.ShapeDtypeStruct((2, g, 2, p), F32)],
        compiler_params=_cparams("arbitrary", "arbitrary"),
        name="s5_operators",
    )(lam_row, lam_col, ls, bt_re, bt_im, ct_re, ct_im)

    k5 = k.reshape(2, g, t, i, i)
    s_idx = np.arange(t)[:, None]
    t_idx = np.arange(t)[None, :]

    def toeplitz(kd, lag):
        m = kd[:, np.clip(lag, 0, t - 1)]
        m = jnp.where((lag >= 0)[None, :, :, None, None], m, 0.0)
        return m.transpose(0, 1, 4, 2, 3).reshape(g, rows, rows)

    m = jnp.stack([toeplitz(k5[0], t_idx - s_idx), toeplitz(k5[1], s_idx - t_idx)])
    return m, w_re, w_im, g_re, g_im, a


def _s5_scan_kernel(bsz, u_ref, d_ref, m_ref, w_re_ref, w_im_ref, g_re_ref, g_im_ref, a_ref,
                    h0_re_ref, h0_im_ref, y_ref, hf_re_ref, hf_im_ref,
                    z_re_scr, z_im_scr, hp_re_scr, hp_im_scr):
    backward = pl.program_id(1) == 1
    u = u_ref[0]
    n_chunks = u.shape[0] // bsz
    z_re_scr[...] = _dot_nt(u, w_re_ref[0, 0])
    z_im_scr[...] = _dot_nt(u, w_im_ref[0, 0])
    a = a_ref[0, 0]
    a_re, a_im = a[0:1, :], a[1:2, :]

    def chunk_step(k, carry):
        h_re, h_im = carry
        c = jnp.where(backward, n_chunks - 1 - k, k)
        rows = pl.ds(pl.multiple_of(c * bsz, bsz), bsz)
        hp_re_scr[rows, :] = h_re
        hp_im_scr[rows, :] = h_im
        n_re = a_re * h_re - a_im * h_im + z_re_scr[rows, :]
        n_im = a_re * h_im + a_im * h_re + z_im_scr[rows, :]
        return n_re, n_im

    h_re, h_im = lax.fori_loop(0, n_chunks, chunk_step, (h0_re_ref[0, 0], h0_im_ref[0, 0]))
    hf_re_ref[0, 0] = h_re
    hf_im_ref[0, 0] = h_im

    y = (_dot(u, m_ref[0, 0]) + _dot_nt(hp_re_scr[...], g_re_ref[0, 0])
         - _dot_nt(hp_im_scr[...], g_im_ref[0, 0]))

    @pl.when(jnp.logical_not(backward))
    def _():
        y_ref[0] = d_ref[0] * u + y

    @pl.when(backward)
    def _():
        y_ref[0] = y_ref[0] + y


def _s5_bidir(u, ops, d_skip, h0):
    m, w_re, w_im, g_re, g_im, a = ops
    bsz, n, width = u.shape
    _, g, p, rows = w_re.shape
    t = STEP_CHUNK
    i = rows // t
    nc = n // t
    r = nc * bsz
    ug = u.reshape(bsz, nc, t, g, i).transpose(3, 1, 0, 2, 4).reshape(g, r, rows)
    dg = jnp.tile(d_skip.reshape(g, 1, i), (1, 1, t))
    if h0 is None:
        h0 = (jnp.zeros((2, g, bsz, p), F32),) * 2
    per_g = lambda *shape: pl.BlockSpec((1,) + shape, lambda gi, dr: (gi, 0, 0))
    per_dg = lambda *shape: pl.BlockSpec((1, 1) + shape, lambda gi, dr: (dr, gi, 0, 0))
    y, hf_re, hf_im = pl.pallas_call(
        functools.partial(_s5_scan_kernel, bsz),
        grid=(g, 2),
        in_specs=[per_g(r, rows), per_g(1, rows), per_dg(rows, rows), per_dg(p, rows), per_dg(p, rows),
                  per_dg(rows, p), per_dg(rows, p), per_dg(2, p), per_dg(bsz, p), per_dg(bsz, p)],
        out_specs=[per_g(r, rows), per_dg(bsz, p), per_dg(bsz, p)],
        out_shape=[jax.ShapeDtypeStruct((g, r, rows), F32),
                   jax.ShapeDtypeStruct((2, g, bsz, p), F32),
                   jax.ShapeDtypeStruct((2, g, bsz, p), F32)],
        scratch_shapes=[pltpu.VMEM((r, p), F32)] * 4,
        compiler_params=_cparams("arbitrary", "arbitrary"),
        name="s5_scan",
    )(ug, dg, m, w_re, w_im, g_re, g_im, a, h0[0], h0[1])
    y = y.reshape(g, nc, bsz, t, i).transpose(2, 1, 3, 0, 4).reshape(bsz, n, width)
    return y, (hf_re, hf_im)


def _even_post_kernel(seg_len, alpha, ys_ref, pc_ref, h_ref, g1_ref, wout_ref, gluw_ref, glub_ref,
                      cw_ref, cb_ref, lng_ref, lnb_ref, o_ref):
    ys = ys_ref[0]
    tl, sw = ys.shape
    g = _gelu(ys)
    ya = g * jax.nn.sigmoid(_dot(g, gluw_ref[...]) + glub_ref[...])
    cwid = pc_ref.shape[-1] // 3
    z = pc_ref[0, :, cwid:2 * cwid] * pc_ref[0, :, 2 * cwid:]
    pos = lax.broadcasted_iota(jnp.int32, (tl, 1), 0) % seg_len
    z_prev = jnp.where(pos == 0, 0.0, pltpu.roll(z, 1, axis=0))
    z_next = jnp.where(pos == seg_len - 1, 0.0, pltpu.roll(z, tl - 1, axis=0))
    cw = cw_ref[...]
    conv = cw[0:1, :] * z_prev + cw[1:2, :] * z + cw[2:3, :] * z_next + cb_ref[...]
    yb = pc_ref[0, :, :cwid] * conv
    out = _dot(ya, wout_ref[:sw, :]) + _dot(yb, wout_ref[sw:, :])
    h = h_ref[0]
    o_ref[0] = _layer_norm(alpha * h + g1_ref[0] * out, lng_ref[...], lnb_ref[...])


def _even_post(y_ssm, p_conv, h, g1, w_out, glu_w, glu_b, conv_w, conv_b, ln_g, ln_b, seg_len, alpha):
    bsz, n, d = h.shape
    sw = y_ssm.shape[-1]
    cw3 = p_conv.shape[-1]
    cwid = cw3 // 3
    tl = _token_tile(n, 512)
    assert tl % seg_len == 0
    tok = lambda b, j: (b, j, 0)
    mod = lambda b, j: (b, 0, 0)
    full = lambda *shape: pl.BlockSpec(shape, lambda b, j: (0,) * len(shape))
    return pl.pallas_call(
        functools.partial(_even_post_kernel, seg_len, alpha),
        grid=(bsz, n // tl),
        in_specs=[pl.BlockSpec((1, tl, sw), tok), pl.BlockSpec((1, tl, cw3), tok),
                  pl.BlockSpec((1, tl, d), tok), pl.BlockSpec((1, 1, d), mod),
                  full(sw + cwid, d), full(sw, sw), full(1, sw), full(3, cwid), full(1, cwid),
                  full(1, d), full(1, d)],
        out_specs=pl.BlockSpec((1, tl, d), tok),
        out_shape=jax.ShapeDtypeStruct((bsz, n, d), F32),
        compiler_params=_cparams("arbitrary", "arbitrary"),
        name="even_post",
    )(y_ssm, p_conv, h, g1, w_out, glu_w, glu_b.reshape(1, sw), conv_w, conv_b.reshape(1, cwid),
      ln_g.reshape(1, d), ln_b.reshape(1, d))


def _pool_matrices(seg_len, tile):
    pos = np.arange(tile) % seg_len
    seg = np.arange(tile) // seg_len
    mats = []
    for w in POOL_WINDOWS:
        lo = np.clip(pos - w // 2, 0, seg_len - 1)
        hi = np.clip(pos + w // 2 - 1, 0, seg_len - 1)
        cnt = (hi - lo + 1).astype(np.float64)
        inside = ((seg[:, None] == seg[None, :]) & (pos[None, :] >= lo[:, None])
                  & (pos[None, :] <= hi[:, None]))
        mats.append(inside / cnt[:, None] - np.eye(tile))
    return np.stack(mats).astype(np.float32)


def _pool_kernel(alpha, h_ref, sc_ref, sh_ref, g1_ref, pm_ref, pw_ref, ps_ref, lng_ref, lnb_ref, o_ref):
    h = h_ref[0]
    a = h * (1.0 + sc_ref[0]) + sh_ref[0]
    n_win, pg, _ = pw_ref.shape
    ys = []
    for gi in range(n_win):
        z = _dot(pm_ref[gi], a[:, gi * pg:(gi + 1) * pg])
        ys.append(_dot(z, pw_ref[gi]))
    y = jnp.concatenate(ys, axis=-1) * ps_ref[...]
    o_ref[0] = _layer_norm(alpha * h + g1_ref[0] * y, lng_ref[...], lnb_ref[...])


def _pool_mixer(h, sc, sh, g1, pool_w, pool_scale, ln_g, ln_b, seg_len, alpha):
    bsz, n, d = h.shape
    tl = _token_tile(n, 256)
    assert tl % seg_len == 0
    pm = jnp.asarray(_pool_matrices(seg_len, tl))
    n_win, pg, _ = pool_w.shape
    tok = lambda b, j: (b, j, 0)
    mod = lambda b, j: (b, 0, 0)
    full = lambda *shape: pl.BlockSpec(shape, lambda b, j: (0,) * len(shape))
    return pl.pallas_call(
        functools.partial(_pool_kernel, alpha),
        grid=(bsz, n // tl),
        in_specs=[pl.BlockSpec((1, tl, d), tok), pl.BlockSpec((1, 1, d), mod), pl.BlockSpec((1, 1, d), mod),
                  pl.BlockSpec((1, 1, d), mod), full(n_win, tl, tl), full(n_win, pg, pg), full(1, d),
                  full(1, d), full(1, d)],
        out_specs=pl.BlockSpec((1, tl, d), tok),
        out_shape=jax.ShapeDtypeStruct((bsz, n, d), F32),
        compiler_params=_cparams("arbitrary", "arbitrary"),
        name="pool_mixer",
    )(h, sc, sh, g1, pm, pool_w, pool_scale.reshape(1, d), ln_g.reshape(1, d), ln_b.reshape(1, d))


def _top_rows(s, n_top):
    n = s.shape[0]
    rid = lax.broadcasted_iota(jnp.int32, s.shape, 0).astype(F32)
    vals, rows = [], []
    for _ in range(n_top):
        m = jnp.max(s, axis=0, keepdims=True)
        sel = jnp.min(jnp.where(s == m, rid, float(n)), axis=0, keepdims=True)
        vals.append(m)
        rows.append(sel)
        s = jnp.where(rid == sel, -jnp.inf, s)
    return jnp.concatenate(vals, axis=0), jnp.concatenate(rows, axis=0)


def _candidate_slabs(k_top):
    slabs = []
    for a in range(k_top):
        n_valid = k_top // (a + 1)
        if n_valid == 1:
            slabs.append((a, k_top, 1, 1))
            break
        slabs.append((a, a + 1, -(-n_valid // 8) * 8, n_valid))
    return slabs


def _peer_route_kernel(n_heads, h_ref, sc_ref, sh_ref, wq_ref, keys_ref, idx_ref, gate_ref):
    xin = (h_ref[0] * (1.0 + sc_ref[0]) + sh_ref[0]).astype(jnp.bfloat16)
    n_keys, half = keys_ref.shape[1], keys_ref.shape[2]
    k_top = PEER_TOPK
    slabs = _candidate_slabs(k_top)
    for head in range(n_heads):
        tops = []
        for side in range(2):
            col = (head * 2 + side) * half
            q = jnp.dot(xin, wq_ref[:, col:col + half], preferred_element_type=F32)
            s = _dot_nt(keys_ref[side], q)
            tops.append(_top_rows(s, k_top))
        (v0, i0), (v1, i1) = tops
        cs, ci = [], []
        for a_lo, a_hi, n_b, n_valid in slabs:
            b_hi = 1 if n_b == 1 else n_b
            sl = v0[a_lo:a_hi] + v1[0:b_hi]
            if n_valid < n_b:
                sl = jnp.where(lax.broadcasted_iota(jnp.int32, sl.shape, 0) < n_valid, sl, -jnp.inf)
            cs.append(sl)
            ci.append(i0[a_lo:a_hi] * float(n_keys) + i1[0:b_hi])
        cand_s = jnp.concatenate(cs, axis=0)
        cand_i = jnp.concatenate(ci, axis=0)
        n_cand = cand_s.shape[0]
        rid = lax.broadcasted_iota(jnp.int32, cand_s.shape, 0).astype(F32)
        best_s, best_i = [], []
        s = cand_s
        for _ in range(k_top):
            m = jnp.max(s, axis=0, keepdims=True)
            sel = jnp.min(jnp.where(s == m, rid, float(n_cand)), axis=0, keepdims=True)
            hit = rid == sel
            best_s.append(m)
            best_i.append(jnp.sum(jnp.where(hit, cand_i, 0.0), axis=0, keepdims=True))
            s = jnp.where(hit, -jnp.inf, s)
        bs = jnp.concatenate(best_s, axis=0)
        e = jnp.exp(bs - bs[0:1, :])
        gate = e / jnp.sum(e, axis=0, keepdims=True)
        idx_ref[0, head * k_top:(head + 1) * k_top, :] = jnp.concatenate(best_i, axis=0).astype(jnp.int32)
        gate_ref[0, head * k_top:(head + 1) * k_top, :] = gate


def _peer_route(h, sc, sh, wq, keys):
    bsz, n, d = h.shape
    _, n_keys, half = keys.shape
    n_heads = wq.shape[1] // (2 * half)
    tt = _token_tile(n, 256)
    hk = n_heads * PEER_TOPK
    mod = lambda b, j: (b, 0, 0)
    return pl.pallas_call(
        functools.partial(_peer_route_kernel, n_heads),
        grid=(bsz, n // tt),
        in_specs=[pl.BlockSpec((1, tt, d), lambda b, j: (b, j, 0)),
                  pl.BlockSpec((1, 1, d), mod), pl.BlockSpec((1, 1, d), mod),
                  pl.BlockSpec(wq.shape, lambda b, j: (0, 0)),
                  pl.BlockSpec(keys.shape, lambda b, j: (0, 0, 0))],
        out_specs=[pl.BlockSpec((1, hk, tt), lambda b, j: (b, 0, j)),
                   pl.BlockSpec((1, hk, tt), lambda b, j: (b, 0, j))],
        out_shape=[jax.ShapeDtypeStruct((bsz, hk, n), jnp.int32),
                   jax.ShapeDtypeStruct((bsz, hk, n), F32)],
        compiler_params=_cparams("arbitrary", "arbitrary"),
        name="peer_route",
    )(h, sc, sh, wq.astype(jnp.bfloat16), keys)


EXPERT_SLOTS = 4
PREFETCH_TOKENS = EXPERT_SLOTS - 1
assert PREFETCH_TOKENS <= V7X_SUBLANES


def _expert_pitch(rows):
    tiles = -(-rows // V7X_SUBLANES)
    return V7X_SUBLANES * (tiles if tiles % 2 else tiles + 1)


def _peer_expert_kernel(alpha, pitch, idx_ref, gate_ref, h_ref, sc_ref, sh_ref, g2_ref, lng_ref, lnb_ref,
                        uv_hbm, o_ref, *scratch):
    slots, (xin_scr, f_scr, sems) = scratch[:EXPERT_SLOTS], scratch[EXPERT_SLOTS:]
    tb, n_sel = gate_ref.shape[1], gate_ref.shape[2]
    h = h_ref[0]
    d = h.shape[-1]
    lanes = slots[0].shape[-1]
    ru = d // lanes
    rows = 2 * ru
    per_stage = n_sel // rows
    xin_scr[...] = h * (1.0 + sc_ref[0]) + sh_ref[0]
    step = pl.program_id(0) * pl.num_programs(1) + pl.program_id(1)
    n_steps = pl.num_programs(0) * pl.num_programs(1)

    def issue(t, slot, stage):
        for k in range(stage * per_stage, (stage + 1) * per_stage):
            e = idx_ref[t, k]
            src = uv_hbm.at[pl.ds(pl.multiple_of(e * rows, rows), rows), :]
            dst = slots[slot].at[pl.ds(k * pitch, rows), :]
            pltpu.make_async_copy(src, dst, sems.at[slot]).start(priority=k % 2)

    def wait(slot):
        total = n_sel * rows
        pltpu.make_async_copy(uv_hbm.at[pl.ds(0, total), :], slots[slot].at[pl.ds(0, total), :],
                              sems.at[slot]).wait()

    eye = (lax.broadcasted_iota(jnp.int32, (n_sel, n_sel), 0)
           == lax.broadcasted_iota(jnp.int32, (n_sel, n_sel), 1))

    @pl.when(step == 0)
    def _():
        for t0 in range(PREFETCH_TOKENS):
            for stage in range(rows):
                issue(t0, t0 % EXPERT_SLOTS, stage)

    def token(t, slot):
        ahead = (slot + PREFETCH_TOKENS) % EXPERT_SLOTS
        buf = slots[slot]
        wait(slot)
        x = xin_scr[pl.ds(t, 1), :]
        dots = None
        for s in range(ru):
            issue(t + PREFETCH_TOKENS, ahead, s)
            u_s = buf[pl.ds(s, n_sel, stride=pitch), :]
            part = u_s * x[:, s * lanes:(s + 1) * lanes]
            dots = part if dots is None else dots + part
        g_row = gate_ref[0, pl.ds(t, 1), :]
        g_col = jnp.sum(jnp.where(eye, g_row, 0.0), axis=1, keepdims=True)
        coef = g_col * _gelu(jnp.sum(dots, axis=1, keepdims=True))
        f_parts = []
        for s in range(ru):
            issue(t + PREFETCH_TOKENS, ahead, ru + s)
            v_s = buf[pl.ds(ru + s, n_sel, stride=pitch), :]
            f_parts.append(jnp.sum(coef * v_s, axis=0, keepdims=True))
        f_scr[pl.ds(t, 1), :] = jnp.concatenate(f_parts, axis=1)

    def token_group(g, carry):
        for slot in range(EXPERT_SLOTS):
            token(g * EXPERT_SLOTS + slot, slot)
        return carry

    lax.fori_loop(0, tb // EXPERT_SLOTS, token_group, 0)

    @pl.when(step == n_steps - 1)
    def _():
        for t0 in range(PREFETCH_TOKENS):
            wait((tb + t0) % EXPERT_SLOTS)

    o_ref[0] = _layer_norm(alpha * h + g2_ref[0] * f_scr[...], lng_ref[...], lnb_ref[...])


def _expert_table(u_tab, v_tab):
    n_exp, d = u_tab.shape
    ru = d // V7X_LANES
    uv = jnp.concatenate([u_tab.reshape(n_exp, ru, V7X_LANES), v_tab.reshape(n_exp, ru, V7X_LANES)], axis=1)
    return uv.reshape(n_exp * 2 * ru, V7X_LANES)


def _peer_experts(idx, gate, h, sc, sh, g2, ln_g, ln_b, uv_tab, alpha):
    bsz, n, d = h.shape
    n_sel = idx.shape[-1]
    rows = 2 * d // V7X_LANES
    assert d % V7X_LANES == 0 and n_sel % rows == 0
    pitch = _expert_pitch(rows)
    tb = _token_tile(n, 64)
    assert tb % EXPERT_SLOTS == 0
    nb = n // tb
    idx_blk = idx.reshape(bsz * nb, tb, n_sel)
    first = idx_blk[:, :V7X_SUBLANES, :]
    halo = jnp.concatenate([first[1:], first[-1:]], axis=0)
    idx_rows = jnp.concatenate([idx_blk, halo], axis=1).reshape(bsz * nb * (tb + V7X_SUBLANES), n_sel)
    tok = lambda b, j: (b, j, 0)
    mod = lambda b, j: (b, 0, 0)
    full = lambda *shape: pl.BlockSpec(shape, lambda b, j: (0,) * len(shape))
    return pl.pallas_call(
        functools.partial(_peer_expert_kernel, alpha, pitch),
        grid=(bsz, nb),
        in_specs=[pl.BlockSpec((tb + V7X_SUBLANES, n_sel), lambda b, j: (b * nb + j, 0),
                               memory_space=pltpu.SMEM),
                  pl.BlockSpec((1, tb, n_sel), tok), pl.BlockSpec((1, tb, d), tok),
                  pl.BlockSpec((1, 1, d), mod), pl.BlockSpec((1, 1, d), mod), pl.BlockSpec((1, 1, d), mod),
                  full(1, d), full(1, d),
                  pl.BlockSpec(memory_space=pl.ANY)],
        out_specs=pl.BlockSpec((1, tb, d), tok),
        out_shape=jax.ShapeDtypeStruct((bsz, n, d), F32),
        scratch_shapes=[pltpu.VMEM((n_sel * pitch, V7X_LANES), F32)] * EXPERT_SLOTS
                       + [pltpu.VMEM((tb, d), F32), pltpu.VMEM((tb, d), F32),
                          pltpu.SemaphoreType.DMA((EXPERT_SLOTS,))],
        compiler_params=_cparams("arbitrary", "arbitrary"),
        name="peer_experts",
    )(idx_rows, gate, h, sc, sh, g2, ln_g.reshape(1, d), ln_b.reshape(1, d), uv_tab)


def _peer_block(h, sc2, sh2, g2, wq, keys, uv_tab, ln_g, ln_b, alpha):
    idx, gate = _peer_route(h, sc2, sh2, wq, keys)
    idx = idx.transpose(0, 2, 1)
    gate = gate.transpose(0, 2, 1)
    return _peer_experts(idx, gate, h, sc2, sh2, g2, ln_g, ln_b, uv_tab, alpha)


def kernel(x, c, ctx, c_ctx, ada_w, ada_b, ln_mix_g, ln_mix_b, ln_ffn_g, ln_ffn_b, even_w_in, even_w_out, ssm_lam_re, ssm_lam_im, ssm_log_step, ssm_b_re, ssm_b_im, ssm_c_re, ssm_c_im, ssm_d, glu_w, glu_b, conv_w, conv_b, pool_w, pool_scale, peer_wq, peer_keys, peer_u, peer_v):
    depth, d, _ = ada_w.shape
    bsz, seq, _ = x.shape
    ctx_len = ctx.shape[1]
    ssm_w = ssm_d.shape[-1]
    alpha = (2.0 * depth) ** 0.25

    pad = (-(bsz + 1)) % 8
    c_all = jnp.concatenate([c, c_ctx[None, :], jnp.zeros((pad, d), F32)], axis=0)
    mod = _ada_modulation(c_all, ada_w, ada_b)

    def six(rows):
        return [rows[:, None, k * d:(k + 1) * d] for k in range(6)]

    h_lat, h_ctx = x, ctx
    for layer in range(depth):
        ctx_out = any(j > layer and j % 2 == 0 for j in range(depth))
        even = layer % 2 == 0
        sh1, sc1, g1, sh2, sc2, g2 = six(mod[layer, :bsz])
        need_ctx = ctx_out or even
        if need_ctx:
            csh1, csc1, cg1, csh2, csc2, cg2 = six(jnp.broadcast_to(mod[layer, bsz:bsz + 1], (bsz, 6 * d)))
        lmg, lmb = ln_mix_g[layer], ln_mix_b[layer]
        if even:
            e = layer // 2
            ops = _s5_operators(ssm_lam_re[e], ssm_lam_im[e], ssm_log_step[e], ssm_b_re[e], ssm_b_im[e],
                                ssm_c_re[e], ssm_c_im[e])
            post = functools.partial(_even_post, w_out=even_w_out[e], glu_w=glu_w[e], glu_b=glu_b[e],
                                     conv_w=conv_w[e], conv_b=conv_b[e], ln_g=lmg, ln_b=lmb, alpha=alpha)
            ps_ctx, pc_ctx = _in_projection(h_ctx, csc1, csh1, even_w_in[e], ssm_w)
            y_ctx, finals = _s5_bidir(ps_ctx, ops, ssm_d[e], None)
            ps_lat, pc_lat = _in_projection(h_lat, sc1, sh1, even_w_in[e], ssm_w)
            y_lat, _ = _s5_bidir(ps_lat, ops, ssm_d[e], finals)
            h_lat = post(y_lat, pc_lat, h_lat, g1, seg_len=GRID_W)
            if ctx_out:
                h_ctx = post(y_ctx, pc_ctx, h_ctx, cg1, seg_len=ctx_len)
        else:
            o = layer // 2
            h_lat = _pool_mixer(h_lat, sc1, sh1, g1, pool_w[o], pool_scale[o], lmg, lmb, GRID_W, alpha)
            if ctx_out:
                h_ctx = _pool_mixer(h_ctx, csc1, csh1, cg1, pool_w[o], pool_scale[o], lmg, lmb, ctx_len, alpha)
        uv_tab = _expert_table(peer_u[layer], peer_v[layer])
        peer = functools.partial(_peer_block, wq=peer_wq[layer], keys=peer_keys[layer], uv_tab=uv_tab,
                                 ln_g=ln_ffn_g[layer], ln_b=ln_ffn_b[layer], alpha=alpha)
        h_lat = peer(h_lat, sc2, sh2, g2)
        if ctx_out:
            h_ctx = peer(h_ctx, csc2, csh2, cg2)
    return h_lat
```

```python
import functools
import math

import numpy as np
import jax
import jax.numpy as jnp
from jax import lax
from jax.experimental import pallas as pl
from jax.experimental.pallas import tpu as pltpu

F32 = jnp.float32
HI = lax.Precision.HIGHEST

GRID_W = 64
POOL_WINDOWS = (2, 4, 8, 16)
PEER_TOPK = 16
LN_EPS = 1e-5
STEP_CHUNK = 64
VMEM_LIMIT_V7X = 56 * 1024 * 1024
V7X_SUBLANES, V7X_LANES = 8, 128
NT_DIMS = (((1,), (1,)), ((), ()))


def _cparams(*sem):
    return pltpu.CompilerParams(dimension_semantics=sem, vmem_limit_bytes=VMEM_LIMIT_V7X)


def _dot(a, b):
    return jnp.dot(a, b, preferred_element_type=F32, precision=HI)


def _dot_bf16(a, w):
    return jnp.dot(a.astype(jnp.bfloat16), w, preferred_element_type=F32)


def _dot_nt(a, b):
    return lax.dot_general(a, b, NT_DIMS, preferred_element_type=F32, precision=HI)


def _gelu(x):
    return 0.5 * x * (1.0 + lax.erf(x * (1.0 / math.sqrt(2.0))))


def _layer_norm(x, g, b):
    mu = jnp.mean(x, axis=-1, keepdims=True)
    xc = x - mu
    var = jnp.mean(xc * xc, axis=-1, keepdims=True)
    return xc * lax.rsqrt(var + LN_EPS) * g + b


def _token_tile(n, cap):
    t = min(n, cap)
    assert n % t == 0
    return t


def _ada_kernel(c_ref, w_ref, b_ref, o_ref):
    c = c_ref[...]
    s = c * jax.nn.sigmoid(c)
    o_ref[0] = _dot(s, w_ref[0]) + b_ref[0]


def _ada_modulation(c_all, ada_w, ada_b):
    depth, d, e = ada_w.shape
    rows = c_all.shape[0]
    te = e // 6
    return pl.pallas_call(
        _ada_kernel,
        grid=(depth, e // te),
        in_specs=[pl.BlockSpec((rows, d), lambda l, j: (0, 0)),
                  pl.BlockSpec((1, d, te), lambda l, j: (l, 0, j)),
                  pl.BlockSpec((1, 1, te), lambda l, j: (l, 0, j))],
        out_specs=pl.BlockSpec((1, rows, te), lambda l, j: (l, 0, j)),
        out_shape=jax.ShapeDtypeStruct((depth, rows, e), F32),
        compiler_params=_cparams("arbitrary", "arbitrary"),
        name="ada_modulation",
    )(c_all, ada_w, ada_b.reshape(depth, 1, e))


def _inproj_kernel(ssm_w, h_ref, sc_ref, sh_ref, w_ref, ps_ref, pc_ref):
    a = h_ref[0] * (1.0 + sc_ref[0]) + sh_ref[0]
    a = a.astype(jnp.bfloat16)
    ps_ref[0] = _dot_bf16(a, w_ref[:, :ssm_w])
    pc_ref[0] = _dot_bf16(a, w_ref[:, ssm_w:])


def _in_projection(h, sc, sh, w_in, ssm_w):
    bsz, n, d = h.shape
    e = w_in.shape[1]
    tl = _token_tile(n, 512)
    tok = lambda b, j: (b, j, 0)
    mod = lambda b, j: (b, 0, 0)
    return pl.pallas_call(
        functools.partial(_inproj_kernel, ssm_w),
        grid=(bsz, n // tl),
        in_specs=[pl.BlockSpec((1, tl, d), tok),
                  pl.BlockSpec((1, 1, d), mod),
                  pl.BlockSpec((1, 1, d), mod),
                  pl.BlockSpec((d, e), lambda b, j: (0, 0))],
        out_specs=[pl.BlockSpec((1, tl, ssm_w), tok),
                   pl.BlockSpec((1, tl, e - ssm_w), tok)],
        out_shape=[jax.ShapeDtypeStruct((bsz, n, ssm_w), F32),
                   jax.ShapeDtypeStruct((bsz, n, e - ssm_w), F32)],
        compiler_params=_cparams("arbitrary", "arbitrary"),
        name="even_in_projection",
    )(h, sc, sh, w_in.astype(jnp.bfloat16))


def _s5_param_kernel(chunk, lam_row_ref, lam_col_ref, ls_ref, bt_re_ref, bt_im_ref,
                     ct_re_ref, ct_im_ref, k_ref, w_re_ref, w_im_ref, g_re_ref, g_im_ref, a_ref):
    backward = pl.program_id(0) == 1
    p_dim = lam_row_ref.shape[-1]
    i_dim = k_ref.shape[-1]
    rows = chunk * i_dim
    step = jnp.exp(ls_ref[0, 0])
    lam_row = lam_row_ref[0, 0]
    lr_row, li_row = lam_row[0:1, :], lam_row[1:2, :]
    lam_col = lam_col_ref[0, 0]
    lr_col, li_col = lam_col[:, 0:1], lam_col[:, 1:2]
    ldr_row, ldi_row = lr_row * step, li_row * step
    ldr_col, ldi_col = lr_col * step, li_col * step

    def power_rows(expo):
        mag = jnp.exp(expo * ldr_row)
        return mag * jnp.cos(expo * ldi_row), mag * jnp.sin(expo * ldi_row)

    mag = jnp.exp(ldr_col)
    ab_re, ab_im = mag * jnp.cos(ldi_col), mag * jnp.sin(ldi_col)
    den = lr_col * lr_col + li_col * li_col
    nr = ab_re - 1.0
    f_re = (nr * lr_col + ab_im * li_col) / den
    f_im = (ab_im * lr_col - nr * li_col) / den
    bt_re, bt_im = bt_re_ref[0, 0], bt_im_ref[0, 0]
    bbt_re = f_re * bt_re - f_im * bt_im
    bbt_im = f_re * bt_im + f_im * bt_re
    bb_re, bb_im = bbt_re[:, :i_dim], bbt_im[:, :i_dim]

    ct_re, ct_im = ct_re_ref[0, 0], ct_im_ref[0, 0]
    tok = (lax.broadcasted_iota(jnp.int32, (rows, 1), 0) // i_dim).astype(F32)

    e_re, e_im = power_rows(tok)
    ck_re = ct_re * e_re - ct_im * e_im
    ck_im = ct_re * e_im + ct_im * e_re
    k_ref[0, 0] = _dot(ck_re, bb_re) - _dot(ck_im, bb_im)

    f_exp = jnp.where(backward, float(chunk) - tok, tok + 1.0)
    e_re, e_im = power_rows(f_exp)
    g_re_ref[0, 0] = ct_re * e_re - ct_im * e_im
    g_im_ref[0, 0] = ct_re * e_im + ct_im * e_re

    tok_l = (lax.broadcasted_iota(jnp.int32, (1, rows), 1) // i_dim).astype(F32)
    e_exp = jnp.where(backward, tok_l, float(chunk - 1) - tok_l)
    magw = jnp.exp(ldr_col * e_exp)
    ew_re, ew_im = magw * jnp.cos(ldi_col * e_exp), magw * jnp.sin(ldi_col * e_exp)
    w_re_ref[0, 0] = ew_re * bbt_re - ew_im * bbt_im
    w_im_ref[0, 0] = ew_re * bbt_im + ew_im * bbt_re

    full = jnp.full((1, 1), float(chunk), F32)
    a_re, a_im = power_rows(full)
    a_ref[0, 0] = jnp.concatenate([a_re, a_im], axis=0)


def _s5_operators(lam_re, lam_im, log_step, b_re, b_im, c_re, c_im):
    _, g, p = lam_re.shape
    i = b_re.shape[-1]
    t = STEP_CHUNK
    rows = t * i
    lam_row = jnp.stack([lam_re, lam_im], axis=2)
    lam_col = jnp.stack([lam_re, lam_im], axis=3)
    ls = log_step.reshape(2, g, 1, 1)
    bt_re = jnp.tile(b_re, (1, 1, 1, t))
    bt_im = jnp.tile(b_im, (1, 1, 1, t))
    ct_re = jnp.tile(c_re, (1, 1, t, 1))
    ct_im = jnp.tile(c_im, (1, 1, t, 1))
    blk = lambda *shape: pl.BlockSpec((1, 1) + shape, lambda dr, gi: (dr, gi, 0, 0))
    k, w_re, w_im, g_re, g_im, a = pl.pallas_call(
        functools.partial(_s5_param_kernel, t),
        grid=(2, g),
        in_specs=[blk(2, p), blk(p, 2), blk(1, 1), blk(p, rows), blk(p, rows),
                  blk(rows, p), blk(rows, p)],
        out_specs=[blk(rows, i), blk(p, rows), blk(p, rows), blk(rows, p), blk(rows, p), blk(2, p)],
        out_shape=[jax.ShapeDtypeStruct((2, g, rows, i), F32),
                   jax.ShapeDtypeStruct((2, g, p, rows), F32),
                   jax.ShapeDtypeStruct((2, g, p, rows), F32),
                   jax.ShapeDtypeStruct((2, g, rows, p), F32),
                   jax.ShapeDtypeStruct((2, g, rows, p), F32),
                   jax.ShapeDtypeStruct((2, g, 2, p), F32)],
        compiler_params=_cparams("arbitrary", "arbitrary"),
        name="s5_operators",
    )(lam_row, lam_col, ls, bt_re, bt_im, ct_re, ct_im)

    k5 = k.reshape(2, g, t, i, i)
    s_idx = np.arange(t)[:, None]
    t_idx = np.arange(t)[None, :]

    def toeplitz(kd, lag):
        m = kd[:, np.clip(lag, 0, t - 1)]
        m = jnp.where((lag >= 0)[None, :, :, None, None], m, 0.0)
        return m.transpose(0, 1, 4, 2, 3).reshape(g, rows, rows)

    m = jnp.stack([toeplitz(k5[0], t_idx - s_idx), toeplitz(k5[1], s_idx - t_idx)])
    return m, w_re, w_im, g_re, g_im, a


def _s5_scan_kernel(bsz, u_ref, d_ref, m_ref, w_re_ref, w_im_ref, g_re_ref, g_im_ref, a_ref,
                    h0_re_ref, h0_im_ref, y_ref, hf_re_ref, hf_im_ref,
                    z_re_scr, z_im_scr, hp_re_scr, hp_im_scr):
    backward = pl.program_id(1) == 1
    u = u_ref[0]
    n_chunks = u.shape[0] // bsz
    z_re_scr[...] = _dot_nt(u, w_re_ref[0, 0])
    z_im_scr[...] = _dot_nt(u, w_im_ref[0, 0])
    a = a_ref[0, 0]
    a_re, a_im = a[0:1, :], a[1:2, :]

    def chunk_step(k, carry):
        h_re, h_im = carry
        c = jnp.where(backward, n_chunks - 1 - k, k)
        rows = pl.ds(pl.multiple_of(c * bsz, bsz), bsz)
        hp_re_scr[rows, :] = h_re
        hp_im_scr[rows, :] = h_im
        n_re = a_re * h_re - a_im * h_im + z_re_scr[rows, :]
        n_im = a_re * h_im + a_im * h_re + z_im_scr[rows, :]
        return n_re, n_im

    h_re, h_im = lax.fori_loop(0, n_chunks, chunk_step, (h0_re_ref[0, 0], h0_im_ref[0, 0]))
    hf_re_ref[0, 0] = h_re
    hf_im_ref[0, 0] = h_im

    y = (_dot(u, m_ref[0, 0]) + _dot_nt(hp_re_scr[...], g_re_ref[0, 0])
         - _dot_nt(hp_im_scr[...], g_im_ref[0, 0]))

    @pl.when(jnp.logical_not(backward))
    def _():
        y_ref[0] = d_ref[0] * u + y

    @pl.when(backward)
    def _():
        y_ref[0] = y_ref[0] + y


def _s5_bidir(u, ops, d_skip, h0):
    m, w_re, w_im, g_re, g_im, a = ops
    bsz, n, width = u.shape
    _, g, p, rows = w_re.shape
    t = STEP_CHUNK
    i = rows // t
    nc = n // t
    r = nc * bsz
    ug = u.reshape(bsz, nc, t, g, i).transpose(3, 1, 0, 2, 4).reshape(g, r, rows)
    dg = jnp.tile(d_skip.reshape(g, 1, i), (1, 1, t))
    if h0 is None:
        h0 = (jnp.zeros((2, g, bsz, p), F32),) * 2
    per_g = lambda *shape: pl.BlockSpec((1,) + shape, lambda gi, dr: (gi, 0, 0))
    per_dg = lambda *shape: pl.BlockSpec((1, 1) + shape, lambda gi, dr: (dr, gi, 0, 0))
    y, hf_re, hf_im = pl.pallas_call(
        functools.partial(_s5_scan_kernel, bsz),
        grid=(g, 2),
        in_specs=[per_g(r, rows), per_g(1, rows), per_dg(rows, rows), per_dg(p, rows), per_dg(p, rows),
                  per_dg(rows, p), per_dg(rows, p), per_dg(2, p), per_dg(bsz, p), per_dg(bsz, p)],
        out_specs=[per_g(r, rows), per_dg(bsz, p), per_dg(bsz, p)],
        out_shape=[jax.ShapeDtypeStruct((g, r, rows), F32),
                   jax.ShapeDtypeStruct((2, g, bsz, p), F32),
                   jax.ShapeDtypeStruct((2, g, bsz, p), F32)],
        scratch_shapes=[pltpu.VMEM((r, p), F32)] * 4,
        compiler_params=_cparams("arbitrary", "arbitrary"),
        name="s5_scan",
    )(ug, dg, m, w_re, w_im, g_re, g_im, a, h0[0], h0[1])
    y = y.reshape(g, nc, bsz, t, i).transpose(2, 1, 3, 0, 4).reshape(bsz, n, width)
    return y, (hf_re, hf_im)


def _even_post_kernel(seg_len, alpha, ys_ref, pc_ref, h_ref, g1_ref, wout_ref, gluw_ref, glub_ref,
                      cw_ref, cb_ref, lng_ref, lnb_ref, o_ref):
    ys = ys_ref[0]
    tl, sw = ys.shape
    g = _gelu(ys)
    ya = g * jax.nn.sigmoid(_dot_bf16(g, gluw_ref[...]) + glub_ref[...])
    cwid = pc_ref.shape[-1] // 3
    z = pc_ref[0, :, cwid:2 * cwid] * pc_ref[0, :, 2 * cwid:]
    pos = lax.broadcasted_iota(jnp.int32, (tl, 1), 0) % seg_len
    z_prev = jnp.where(pos == 0, 0.0, pltpu.roll(z, 1, axis=0))
    z_next = jnp.where(pos == seg_len - 1, 0.0, pltpu.roll(z, tl - 1, axis=0))
    cw = cw_ref[...]
    conv = cw[0:1, :] * z_prev + cw[1:2, :] * z + cw[2:3, :] * z_next + cb_ref[...]
    yb = pc_ref[0, :, :cwid] * conv
    out = _dot_bf16(ya, wout_ref[:sw, :]) + _dot_bf16(yb, wout_ref[sw:, :])
    h = h_ref[0]
    o_ref[0] = _layer_norm(alpha * h + g1_ref[0] * out, lng_ref[...], lnb_ref[...])


def _even_post(y_ssm, p_conv, h, g1, w_out, glu_w, glu_b, conv_w, conv_b, ln_g, ln_b, seg_len, alpha):
    bsz, n, d = h.shape
    sw = y_ssm.shape[-1]
    cw3 = p_conv.shape[-1]
    cwid = cw3 // 3
    tl = _token_tile(n, 512)
    assert tl % seg_len == 0
    tok = lambda b, j: (b, j, 0)
    mod = lambda b, j: (b, 0, 0)
    full = lambda *shape: pl.BlockSpec(shape, lambda b, j: (0,) * len(shape))
    return pl.pallas_call(
        functools.partial(_even_post_kernel, seg_len, alpha),
        grid=(bsz, n // tl),
        in_specs=[pl.BlockSpec((1, tl, sw), tok), pl.BlockSpec((1, tl, cw3), tok),
                  pl.BlockSpec((1, tl, d), tok), pl.BlockSpec((1, 1, d), mod),
                  full(sw + cwid, d), full(sw, sw), full(1, sw), full(3, cwid), full(1, cwid),
                  full(1, d), full(1, d)],
        out_specs=pl.BlockSpec((1, tl, d), tok),
        out_shape=jax.ShapeDtypeStruct((bsz, n, d), F32),
        compiler_params=_cparams("arbitrary", "arbitrary"),
        name="even_post",
    )(y_ssm, p_conv, h, g1, w_out.astype(jnp.bfloat16), glu_w.astype(jnp.bfloat16), glu_b.reshape(1, sw), conv_w, conv_b.reshape(1, cwid),
      ln_g.reshape(1, d), ln_b.reshape(1, d))


def _pool_matrices(seg_len, tile):
    pos = np.arange(tile) % seg_len
    seg = np.arange(tile) // seg_len
    mats = []
    for w in POOL_WINDOWS:
        lo = np.clip(pos - w // 2, 0, seg_len - 1)
        hi = np.clip(pos + w // 2 - 1, 0, seg_len - 1)
        cnt = (hi - lo + 1).astype(np.float64)
        inside = ((seg[:, None] == seg[None, :]) & (pos[None, :] >= lo[:, None])
                  & (pos[None, :] <= hi[:, None]))
        mats.append(inside / cnt[:, None] - np.eye(tile))
    return np.stack(mats).astype(np.float32)


def _pool_kernel(alpha, h_ref, sc_ref, sh_ref, g1_ref, pm_ref, pw_ref, ps_ref, lng_ref, lnb_ref, o_ref):
    h = h_ref[0]
    a = h * (1.0 + sc_ref[0]) + sh_ref[0]
    n_win, pg, _ = pw_ref.shape
    ys = []
    for gi in range(n_win):
        z = _dot(pm_ref[gi], a[:, gi * pg:(gi + 1) * pg])
        ys.append(_dot_bf16(z, pw_ref[gi]))
    y = jnp.concatenate(ys, axis=-1) * ps_ref[...]
    o_ref[0] = _layer_norm(alpha * h + g1_ref[0] * y, lng_ref[...], lnb_ref[...])


def _pool_mixer(h, sc, sh, g1, pool_w, pool_scale, ln_g, ln_b, seg_len, alpha):
    bsz, n, d = h.shape
    tl = _token_tile(n, 256)
    assert tl % seg_len == 0
    pm = jnp.asarray(_pool_matrices(seg_len, tl))
    n_win, pg, _ = pool_w.shape
    tok = lambda b, j: (b, j, 0)
    mod = lambda b, j: (b, 0, 0)
    full = lambda *shape: pl.BlockSpec(shape, lambda b, j: (0,) * len(shape))
    return pl.pallas_call(
        functools.partial(_pool_kernel, alpha),
        grid=(bsz, n // tl),
        in_specs=[pl.BlockSpec((1, tl, d), tok), pl.BlockSpec((1, 1, d), mod), pl.BlockSpec((1, 1, d), mod),
                  pl.BlockSpec((1, 1, d), mod), full(n_win, tl, tl), full(n_win, pg, pg), full(1, d),
                  full(1, d), full(1, d)],
        out_specs=pl.BlockSpec((1, tl, d), tok),
        out_shape=jax.ShapeDtypeStruct((bsz, n, d), F32),
        compiler_params=_cparams("arbitrary", "arbitrary"),
        name="pool_mixer",
    )(h, sc, sh, g1, pm, pool_w.astype(jnp.bfloat16), pool_scale.reshape(1, d), ln_g.reshape(1, d), ln_b.reshape(1, d))


def _top_rows(s, n_top):
    n = s.shape[0]
    rid = lax.broadcasted_iota(jnp.int32, s.shape, 0).astype(F32)
    vals, rows = [], []
    for _ in range(n_top):
        m = jnp.max(s, axis=0, keepdims=True)
        sel = jnp.min(jnp.where(s == m, rid, float(n)), axis=0, keepdims=True)
        vals.append(m)
        rows.append(sel)
        s = jnp.where(rid == sel, -jnp.inf, s)
    return jnp.concatenate(vals, axis=0), jnp.concatenate(rows, axis=0)


def _candidate_slabs(k_top):
    slabs = []
    for a in range(k_top):
        n_valid = k_top // (a + 1)
        if n_valid == 1:
            slabs.append((a, k_top, 1, 1))
            break
        slabs.append((a, a + 1, -(-n_valid // 8) * 8, n_valid))
    return slabs


def _peer_route_kernel(n_heads, h_ref, sc_ref, sh_ref, wq_ref, keys_ref, idx_ref, gate_ref):
    xin = (h_ref[0] * (1.0 + sc_ref[0]) + sh_ref[0]).astype(jnp.bfloat16)
    n_keys, half = keys_ref.shape[1], keys_ref.shape[2]
    k_top = PEER_TOPK
    slabs = _candidate_slabs(k_top)
    for head in range(n_heads):
        tops = []
        for side in range(2):
            col = (head * 2 + side) * half
            q = jnp.dot(xin, wq_ref[:, col:col + half], preferred_element_type=F32)
            s = _dot_nt(keys_ref[side], q)
            tops.append(_top_rows(s, k_top))
        (v0, i0), (v1, i1) = tops
        cs, ci = [], []
        for a_lo, a_hi, n_b, n_valid in slabs:
            b_hi = 1 if n_b == 1 else n_b
            sl = v0[a_lo:a_hi] + v1[0:b_hi]
            if n_valid < n_b:
                sl = jnp.where(lax.broadcasted_iota(jnp.int32, sl.shape, 0) < n_valid, sl, -jnp.inf)
            cs.append(sl)
            ci.append(i0[a_lo:a_hi] * float(n_keys) + i1[0:b_hi])
        cand_s = jnp.concatenate(cs, axis=0)
        cand_i = jnp.concatenate(ci, axis=0)
        n_cand = cand_s.shape[0]
        rid = lax.broadcasted_iota(jnp.int32, cand_s.shape, 0).astype(F32)
        best_s, best_i = [], []
        s = cand_s
        for _ in range(k_top):
            m = jnp.max(s, axis=0, keepdims=True)
            sel = jnp.min(jnp.where(s == m, rid, float(n_cand)), axis=0, keepdims=True)
            hit = rid == sel
            best_s.append(m)
            best_i.append(jnp.sum(jnp.where(hit, cand_i, 0.0), axis=0, keepdims=True))
            s = jnp.where(hit, -jnp.inf, s)
        bs = jnp.concatenate(best_s, axis=0)
        e = jnp.exp(bs - bs[0:1, :])
        gate = e / jnp.sum(e, axis=0, keepdims=True)
        idx_ref[0, head * k_top:(head + 1) * k_top, :] = jnp.concatenate(best_i, axis=0).astype(jnp.int32)
        gate_ref[0, head * k_top:(head + 1) * k_top, :] = gate


def _peer_route(h, sc, sh, wq, keys):
    bsz, n, d = h.shape
    _, n_keys, half = keys.shape
    n_heads = wq.shape[1] // (2 * half)
    tt = _token_tile(n, 256)
    hk = n_heads * PEER_TOPK
    mod = lambda b, j: (b, 0, 0)
    return pl.pallas_call(
        functools.partial(_peer_route_kernel, n_heads),
        grid=(bsz, n // tt),
        in_specs=[pl.BlockSpec((1, tt, d), lambda b, j: (b, j, 0)),
                  pl.BlockSpec((1, 1, d), mod), pl.BlockSpec((1, 1, d), mod),
                  pl.BlockSpec(wq.shape, lambda b, j: (0, 0)),
                  pl.BlockSpec(keys.shape, lambda b, j: (0, 0, 0))],
        out_specs=[pl.BlockSpec((1, hk, tt), lambda b, j: (b, 0, j)),
                   pl.BlockSpec((1, hk, tt), lambda b, j: (b, 0, j))],
        out_shape=[jax.ShapeDtypeStruct((bsz, hk, n), jnp.int32),
                   jax.ShapeDtypeStruct((bsz, hk, n), F32)],
        compiler_params=_cparams("arbitrary", "arbitrary"),
        name="peer_route",
    )(h, sc, sh, wq.astype(jnp.bfloat16), keys)


U_HALF_MASK = -65536
EXPERT_SLOTS = 4
PREFETCH_TOKENS = EXPERT_SLOTS - 1
assert PREFETCH_TOKENS <= V7X_SUBLANES


def _expert_pitch(rows):
    tiles = -(-rows // V7X_SUBLANES)
    return V7X_SUBLANES * (tiles if tiles % 2 else tiles + 1)


def _peer_expert_kernel(alpha, pitch, idx_ref, gate_ref, h_ref, sc_ref, sh_ref, g2_ref, lng_ref, lnb_ref,
                        uv_hbm, o_ref, *scratch):
    slots, (xin_scr, f_scr, sems) = scratch[:EXPERT_SLOTS], scratch[EXPERT_SLOTS:]
    tb, n_sel = gate_ref.shape[1], gate_ref.shape[2]
    h = h_ref[0]
    d = h.shape[-1]
    lanes = slots[0].shape[-1]
    ru = d // lanes
    rows = ru
    per_stage = n_sel // (2 * ru)
    xin_scr[...] = h * (1.0 + sc_ref[0]) + sh_ref[0]
    step = pl.program_id(0) * pl.num_programs(1) + pl.program_id(1)
    n_steps = pl.num_programs(0) * pl.num_programs(1)

    def issue(t, slot, stage):
        for k in range(stage * per_stage, (stage + 1) * per_stage):
            e = idx_ref[t, k]
            src = uv_hbm.at[pl.ds(pl.multiple_of(e * rows, rows), rows), :]
            dst = slots[slot].at[pl.ds(k * pitch, rows), :]
            pltpu.make_async_copy(src, dst, sems.at[slot]).start(priority=k % 2)

    def wait(slot):
        total = n_sel * rows
        pltpu.make_async_copy(uv_hbm.at[pl.ds(0, total), :], slots[slot].at[pl.ds(0, total), :],
                              sems.at[slot]).wait()

    eye = (lax.broadcasted_iota(jnp.int32, (n_sel, n_sel), 0)
           == lax.broadcasted_iota(jnp.int32, (n_sel, n_sel), 1))

    @pl.when(step == 0)
    def _():
        for t0 in range(PREFETCH_TOKENS):
            for stage in range(2 * ru):
                issue(t0, t0 % EXPERT_SLOTS, stage)

    def token(t, slot):
        ahead = (slot + PREFETCH_TOKENS) % EXPERT_SLOTS
        buf = slots[slot]
        wait(slot)
        x = xin_scr[pl.ds(t, 1), :]
        dots = None
        for s in range(ru):
            issue(t + PREFETCH_TOKENS, ahead, s)
            words = buf[pl.ds(s, n_sel, stride=pitch), :]
            u_s = lax.bitcast_convert_type(words & U_HALF_MASK, F32)
            part = u_s * x[:, s * lanes:(s + 1) * lanes]
            dots = part if dots is None else dots + part
        g_row = gate_ref[0, pl.ds(t, 1), :]
        g_col = jnp.sum(jnp.where(eye, g_row, 0.0), axis=1, keepdims=True)
        coef = g_col * _gelu(jnp.sum(dots, axis=1, keepdims=True))
        f_parts = []
        for s in range(ru):
            issue(t + PREFETCH_TOKENS, ahead, ru + s)
            words = buf[pl.ds(s, n_sel, stride=pitch), :]
            v_s = lax.bitcast_convert_type(words << 16, F32)
            f_parts.append(jnp.sum(coef * v_s, axis=0, keepdims=True))
        f_scr[pl.ds(t, 1), :] = jnp.concatenate(f_parts, axis=1)

    def token_group(g, carry):
        for slot in range(EXPERT_SLOTS):
            token(g * EXPERT_SLOTS + slot, slot)
        return carry

    lax.fori_loop(0, tb // EXPERT_SLOTS, token_group, 0)

    @pl.when(step == n_steps - 1)
    def _():
        for t0 in range(PREFETCH_TOKENS):
            wait((tb + t0) % EXPERT_SLOTS)

    o_ref[0] = _layer_norm(alpha * h + g2_ref[0] * f_scr[...], lng_ref[...], lnb_ref[...])


def _expert_table(u_tab, v_tab):
    n_exp, d = u_tab.shape

    def bits(tab):
        return lax.bitcast_convert_type(tab.astype(jnp.bfloat16), jnp.uint16).astype(jnp.uint32)

    words = lax.bitcast_convert_type((bits(u_tab) << 16) | bits(v_tab), jnp.int32)
    return words.reshape(n_exp * d // V7X_LANES, V7X_LANES)


def _peer_experts(idx, gate, h, sc, sh, g2, ln_g, ln_b, uv_tab, alpha):
    bsz, n, d = h.shape
    n_sel = idx.shape[-1]
    rows = d // V7X_LANES
    assert d % V7X_LANES == 0 and n_sel % (2 * rows) == 0
    pitch = _expert_pitch(rows)
    tb = _token_tile(n, 64)
    assert tb % EXPERT_SLOTS == 0
    nb = n // tb
    idx_blk = idx.reshape(bsz * nb, tb, n_sel)
    first = idx_blk[:, :V7X_SUBLANES, :]
    halo = jnp.concatenate([first[1:], first[-1:]], axis=0)
    idx_rows = jnp.concatenate([idx_blk, halo], axis=1).reshape(bsz * nb * (tb + V7X_SUBLANES), n_sel)
    tok = lambda b, j: (b, j, 0)
    mod = lambda b, j: (b, 0, 0)
    full = lambda *shape: pl.BlockSpec(shape, lambda b, j: (0,) * len(shape))
    return pl.pallas_call(
        functools.partial(_peer_expert_kernel, alpha, pitch),
        grid=(bsz, nb),
        in_specs=[pl.BlockSpec((tb + V7X_SUBLANES, n_sel), lambda b, j: (b * nb + j, 0),
                               memory_space=pltpu.SMEM),
                  pl.BlockSpec((1, tb, n_sel), tok), pl.BlockSpec((1, tb, d), tok),
                  pl.BlockSpec((1, 1, d), mod), pl.BlockSpec((1, 1, d), mod), pl.BlockSpec((1, 1, d), mod),
                  full(1, d), full(1, d),
                  pl.BlockSpec(memory_space=pl.ANY)],
        out_specs=pl.BlockSpec((1, tb, d), tok),
        out_shape=jax.ShapeDtypeStruct((bsz, n, d), F32),
        scratch_shapes=[pltpu.VMEM((n_sel * pitch, V7X_LANES), jnp.int32)] * EXPERT_SLOTS
                       + [pltpu.VMEM((tb, d), F32), pltpu.VMEM((tb, d), F32),
                          pltpu.SemaphoreType.DMA((EXPERT_SLOTS,))],
        compiler_params=_cparams("arbitrary", "arbitrary"),
        name="peer_experts",
    )(idx_rows, gate, h, sc, sh, g2, ln_g.reshape(1, d), ln_b.reshape(1, d), uv_tab)


def _peer_block(h, sc2, sh2, g2, wq, keys, uv_tab, ln_g, ln_b, alpha):
    idx, gate = _peer_route(h, sc2, sh2, wq, keys)
    idx = idx.transpose(0, 2, 1)
    gate = gate.transpose(0, 2, 1)
    return _peer_experts(idx, gate, h, sc2, sh2, g2, ln_g, ln_b, uv_tab, alpha)


def kernel(x, c, ctx, c_ctx, ada_w, ada_b, ln_mix_g, ln_mix_b, ln_ffn_g, ln_ffn_b, even_w_in, even_w_out, ssm_lam_re, ssm_lam_im, ssm_log_step, ssm_b_re, ssm_b_im, ssm_c_re, ssm_c_im, ssm_d, glu_w, glu_b, conv_w, conv_b, pool_w, pool_scale, peer_wq, peer_keys, peer_u, peer_v):
    depth, d, _ = ada_w.shape
    bsz, seq, _ = x.shape
    ctx_len = ctx.shape[1]
    ssm_w = ssm_d.shape[-1]
    alpha = (2.0 * depth) ** 0.25

    pad = (-(bsz + 1)) % 8
    c_all = jnp.concatenate([c, c_ctx[None, :], jnp.zeros((pad, d), F32)], axis=0)
    mod = _ada_modulation(c_all, ada_w, ada_b)

    def six(rows):
        return [rows[:, None, k * d:(k + 1) * d] for k in range(6)]

    h_lat, h_ctx = x, ctx
    for layer in range(depth):
        ctx_out = any(j > layer and j % 2 == 0 for j in range(depth))
        even = layer % 2 == 0
        sh1, sc1, g1, sh2, sc2, g2 = six(mod[layer, :bsz])
        need_ctx = ctx_out or even
        if need_ctx:
            csh1, csc1, cg1, csh2, csc2, cg2 = six(jnp.broadcast_to(mod[layer, bsz:bsz + 1], (bsz, 6 * d)))
        lmg, lmb = ln_mix_g[layer], ln_mix_b[layer]
        if even:
            e = layer // 2
            ops = _s5_operators(ssm_lam_re[e], ssm_lam_im[e], ssm_log_step[e], ssm_b_re[e], ssm_b_im[e],
                                ssm_c_re[e], ssm_c_im[e])
            post = functools.partial(_even_post, w_out=even_w_out[e], glu_w=glu_w[e], glu_b=glu_b[e],
                                     conv_w=conv_w[e], conv_b=conv_b[e], ln_g=lmg, ln_b=lmb, alpha=alpha)
            ps_ctx, pc_ctx = _in_projection(h_ctx, csc1, csh1, even_w_in[e], ssm_w)
            y_ctx, finals = _s5_bidir(ps_ctx, ops, ssm_d[e], None)
            ps_lat, pc_lat = _in_projection(h_lat, sc1, sh1, even_w_in[e], ssm_w)
            y_lat, _ = _s5_bidir(ps_lat, ops, ssm_d[e], finals)
            h_lat = post(y_lat, pc_lat, h_lat, g1, seg_len=GRID_W)
            if ctx_out:
                h_ctx = post(y_ctx, pc_ctx, h_ctx, cg1, seg_len=ctx_len)
        else:
            o = layer // 2
            h_lat = _pool_mixer(h_lat, sc1, sh1, g1, pool_w[o], pool_scale[o], lmg, lmb, GRID_W, alpha)
            if ctx_out:
                h_ctx = _pool_mixer(h_ctx, csc1, csh1, cg1, pool_w[o], pool_scale[o], lmg, lmb, ctx_len, alpha)
        uv_tab = _expert_table(peer_u[layer], peer_v[layer])
        peer = functools.partial(_peer_block, wq=peer_wq[layer], keys=peer_keys[layer], uv_tab=uv_tab,
                                 ln_g=ln_ffn_g[layer], ln_b=ln_ffn_b[layer], alpha=alpha)
        h_lat = peer(h_lat, sc2, sh2, g2)
        if ctx_out:
            h_ctx = peer(h_ctx, csc2, csh2, cg2)
    return h_lat
```

```python
import functools
import math

import numpy as np
import jax
import jax.numpy as jnp
from jax import lax
from jax.experimental import pallas as pl
from jax.experimental.pallas import tpu as pltpu

F32 = jnp.float32
HI = lax.Precision.HIGHEST

GRID_W = 64
POOL_WINDOWS = (2, 4, 8, 16)
PEER_TOPK = 16
LN_EPS = 1e-5
STEP_CHUNK = 64
VMEM_LIMIT_V7X = 56 * 1024 * 1024
V7X_SUBLANES, V7X_LANES = 8, 128
NT_DIMS = (((1,), (1,)), ((), ()))


def _cparams(*sem):
    return pltpu.CompilerParams(dimension_semantics=sem, vmem_limit_bytes=VMEM_LIMIT_V7X)


def _dot(a, b):
    return jnp.dot(a, b, preferred_element_type=F32, precision=HI)


def _dot_bf16(a, w):
    return jnp.dot(a.astype(jnp.bfloat16), w, preferred_element_type=F32)


def _dot_nt(a, b):
    return lax.dot_general(a, b, NT_DIMS, preferred_element_type=F32, precision=HI)


def _gelu(x):
    return 0.5 * x * (1.0 + lax.erf(x * (1.0 / math.sqrt(2.0))))


def _layer_norm(x, g, b):
    mu = jnp.mean(x, axis=-1, keepdims=True)
    xc = x - mu
    var = jnp.mean(xc * xc, axis=-1, keepdims=True)
    return xc * lax.rsqrt(var + LN_EPS) * g + b


def _token_tile(n, cap):
    t = min(n, cap)
    assert n % t == 0
    return t


def _ada_kernel(c_ref, w_ref, b_ref, o_ref):
    c = c_ref[...]
    s = c * jax.nn.sigmoid(c)
    o_ref[0] = _dot(s, w_ref[0]) + b_ref[0]


def _ada_modulation(c_all, ada_w, ada_b):
    depth, d, e = ada_w.shape
    rows = c_all.shape[0]
    te = e // 6
    return pl.pallas_call(
        _ada_kernel,
        grid=(depth, e // te),
        in_specs=[pl.BlockSpec((rows, d), lambda l, j: (0, 0)),
                  pl.BlockSpec((1, d, te), lambda l, j: (l, 0, j)),
                  pl.BlockSpec((1, 1, te), lambda l, j: (l, 0, j))],
        out_specs=pl.BlockSpec((1, rows, te), lambda l, j: (l, 0, j)),
        out_shape=jax.ShapeDtypeStruct((depth, rows, e), F32),
        compiler_params=_cparams("arbitrary", "arbitrary"),
        name="ada_modulation",
    )(c_all, ada_w, ada_b.reshape(depth, 1, e))


def _inproj_kernel(ssm_w, h_ref, sc_ref, sh_ref, w_ref, ps_ref, pc_ref):
    a = h_ref[0] * (1.0 + sc_ref[0]) + sh_ref[0]
    a = a.astype(jnp.bfloat16)
    ps_ref[0] = _dot_bf16(a, w_ref[:, :ssm_w])
    pc_ref[0] = _dot_bf16(a, w_ref[:, ssm_w:])


def _in_projection(h, sc, sh, w_in, ssm_w):
    bsz, n, d = h.shape
    e = w_in.shape[1]
    tl = _token_tile(n, 512)
    tok = lambda b, j: (b, j, 0)
    mod = lambda b, j: (b, 0, 0)
    return pl.pallas_call(
        functools.partial(_inproj_kernel, ssm_w),
        grid=(bsz, n // tl),
        in_specs=[pl.BlockSpec((1, tl, d), tok),
                  pl.BlockSpec((1, 1, d), mod),
                  pl.BlockSpec((1, 1, d), mod),
                  pl.BlockSpec((d, e), lambda b, j: (0, 0))],
        out_specs=[pl.BlockSpec((1, tl, ssm_w), tok),
                   pl.BlockSpec((1, tl, e - ssm_w), tok)],
        out_shape=[jax.ShapeDtypeStruct((bsz, n, ssm_w), F32),
                   jax.ShapeDtypeStruct((bsz, n, e - ssm_w), F32)],
        compiler_params=_cparams("arbitrary", "arbitrary"),
        name="even_in_projection",
    )(h, sc, sh, w_in.astype(jnp.bfloat16))


def _s5_param_kernel(chunk, lam_row_ref, lam_col_ref, ls_ref, bt_re_ref, bt_im_ref,
                     ct_re_ref, ct_im_ref, k_ref, w_re_ref, w_im_ref, g_re_ref, g_im_ref, a_ref):
    backward = pl.program_id(0) == 1
    p_dim = lam_row_ref.shape[-1]
    i_dim = k_ref.shape[-1]
    rows = chunk * i_dim
    step = jnp.exp(ls_ref[0, 0])
    lam_row = lam_row_ref[0, 0]
    lr_row, li_row = lam_row[0:1, :], lam_row[1:2, :]
    lam_col = lam_col_ref[0, 0]
    lr_col, li_col = lam_col[:, 0:1], lam_col[:, 1:2]
    ldr_row, ldi_row = lr_row * step, li_row * step
    ldr_col, ldi_col = lr_col * step, li_col * step

    def power_rows(expo):
        mag = jnp.exp(expo * ldr_row)
        return mag * jnp.cos(expo * ldi_row), mag * jnp.sin(expo * ldi_row)

    mag = jnp.exp(ldr_col)
    ab_re, ab_im = mag * jnp.cos(ldi_col), mag * jnp.sin(ldi_col)
    den = lr_col * lr_col + li_col * li_col
    nr = ab_re - 1.0
    f_re = (nr * lr_col + ab_im * li_col) / den
    f_im = (ab_im * lr_col - nr * li_col) / den
    bt_re, bt_im = bt_re_ref[0, 0], bt_im_ref[0, 0]
    bbt_re = f_re * bt_re - f_im * bt_im
    bbt_im = f_re * bt_im + f_im * bt_re
    bb_re, bb_im = bbt_re[:, :i_dim], bbt_im[:, :i_dim]

    ct_re, ct_im = ct_re_ref[0, 0], ct_im_ref[0, 0]
    tok = (lax.broadcasted_iota(jnp.int32, (rows, 1), 0) // i_dim).astype(F32)

    e_re, e_im = power_rows(tok)
    ck_re = ct_re * e_re - ct_im * e_im
    ck_im = ct_re * e_im + ct_im * e_re
    k_ref[0, 0] = _dot(ck_re, bb_re) - _dot(ck_im, bb_im)

    f_exp = jnp.where(backward, float(chunk) - tok, tok + 1.0)
    e_re, e_im = power_rows(f_exp)
    g_re_ref[0, 0] = ct_re * e_re - ct_im * e_im
    g_im_ref[0, 0] = ct_re * e_im + ct_im * e_re

    tok_l = (lax.broadcasted_iota(jnp.int32, (1, rows), 1) // i_dim).astype(F32)
    e_exp = jnp.where(backward, tok_l, float(chunk - 1) - tok_l)
    magw = jnp.exp(ldr_col * e_exp)
    ew_re, ew_im = magw * jnp.cos(ldi_col * e_exp), magw * jnp.sin(ldi_col * e_exp)
    w_re_ref[0, 0] = ew_re * bbt_re - ew_im * bbt_im
    w_im_ref[0, 0] = ew_re * bbt_im + ew_im * bbt_re

    full = jnp.full((1, 1), float(chunk), F32)
    a_re, a_im = power_rows(full)
    a_ref[0, 0] = jnp.concatenate([a_re, a_im], axis=0)


def _s5_operators(lam_re, lam_im, log_step, b_re, b_im, c_re, c_im):
    _, g, p = lam_re.shape
    i = b_re.shape[-1]
    t = STEP_CHUNK
    rows = t * i
    lam_row = jnp.stack([lam_re, lam_im], axis=2)
    lam_col = jnp.stack([lam_re, lam_im], axis=3)
    ls = log_step.reshape(2, g, 1, 1)
    bt_re = jnp.tile(b_re, (1, 1, 1, t))
    bt_im = jnp.tile(b_im, (1, 1, 1, t))
    ct_re = jnp.tile(c_re, (1, 1, t, 1))
    ct_im = jnp.tile(c_im, (1, 1, t, 1))
    blk = lambda *shape: pl.BlockSpec((1, 1) + shape, lambda dr, gi: (dr, gi, 0, 0))
    k, w_re, w_im, g_re, g_im, a = pl.pallas_call(
        functools.partial(_s5_param_kernel, t),
        grid=(2, g),
        in_specs=[blk(2, p), blk(p, 2), blk(1, 1), blk(p, rows), blk(p, rows),
                  blk(rows, p), blk(rows, p)],
        out_specs=[blk(rows, i), blk(p, rows), blk(p, rows), blk(rows, p), blk(rows, p), blk(2, p)],
        out_shape=[jax.ShapeDtypeStruct((2, g, rows, i), F32),
                   jax.ShapeDtypeStruct((2, g, p, rows), F32),
                   jax.ShapeDtypeStruct((2, g, p, rows), F32),
                   jax.ShapeDtypeStruct((2, g, rows, p), F32),
                   jax.ShapeDtypeStruct((2, g, rows, p), F32),
                   jax.ShapeDtypeStruct((2, g, 2, p), F32)],
        compiler_params=_cparams("arbitrary", "arbitrary"),
        name="s5_operators",
    )(lam_row, lam_col, ls, bt_re, bt_im, ct_re, ct_im)

    k5 = k.reshape(2, g, t, i, i)
    s_idx = np.arange(t)[:, None]
    t_idx = np.arange(t)[None, :]

    def toeplitz(kd, lag):
        m = kd[:, np.clip(lag, 0, t - 1)]
        m = jnp.where((lag >= 0)[None, :, :, None, None], m, 0.0)
        return m.transpose(0, 1, 4, 2, 3).reshape(g, rows, rows)

    m = jnp.stack([toeplitz(k5[0], t_idx - s_idx), toeplitz(k5[1], s_idx - t_idx)])
    return m, w_re, w_im, g_re, g_im, a


def _s5_scan_kernel(bsz, u_ref, d_ref, m_ref, w_re_ref, w_im_ref, g_re_ref, g_im_ref, a_ref,
                    h0_re_ref, h0_im_ref, y_ref, hf_re_ref, hf_im_ref,
                    z_re_scr, z_im_scr, hp_re_scr, hp_im_scr):
    backward = pl.program_id(1) == 1
    u = u_ref[0]
    n_chunks = u.shape[0] // bsz
    z_re_scr[...] = _dot_nt(u, w_re_ref[0, 0])
    z_im_scr[...] = _dot_nt(u, w_im_ref[0, 0])
    a = a_ref[0, 0]
    a_re, a_im = a[0:1, :], a[1:2, :]

    def chunk_step(k, carry):
        h_re, h_im = carry
        c = jnp.where(backward, n_chunks - 1 - k, k)
        rows = pl.ds(pl.multiple_of(c * bsz, bsz), bsz)
        hp_re_scr[rows, :] = h_re
        hp_im_scr[rows, :] = h_im
        n_re = a_re * h_re - a_im * h_im + z_re_scr[rows, :]
        n_im = a_re * h_im + a_im * h_re + z_im_scr[rows, :]
        return n_re, n_im

    h_re, h_im = lax.fori_loop(0, n_chunks, chunk_step, (h0_re_ref[0, 0], h0_im_ref[0, 0]))
    hf_re_ref[0, 0] = h_re
    hf_im_ref[0, 0] = h_im

    y = (_dot(u, m_ref[0, 0]) + _dot_nt(hp_re_scr[...], g_re_ref[0, 0])
         - _dot_nt(hp_im_scr[...], g_im_ref[0, 0]))

    @pl.when(jnp.logical_not(backward))
    def _():
        y_ref[0] = d_ref[0] * u + y

    @pl.when(backward)
    def _():
        y_ref[0] = y_ref[0] + y


def _s5_bidir(u, ops, d_skip, h0):
    m, w_re, w_im, g_re, g_im, a = ops
    bsz, n, width = u.shape
    _, g, p, rows = w_re.shape
    t = STEP_CHUNK
    i = rows // t
    nc = n // t
    r = nc * bsz
    ug = u.reshape(bsz, nc, t, g, i).transpose(3, 1, 0, 2, 4).reshape(g, r, rows)
    dg = jnp.tile(d_skip.reshape(g, 1, i), (1, 1, t))
    if h0 is None:
        h0 = (jnp.zeros((2, g, bsz, p), F32),) * 2
    per_g = lambda *shape: pl.BlockSpec((1,) + shape, lambda gi, dr: (gi, 0, 0))
    per_dg = lambda *shape: pl.BlockSpec((1, 1) + shape, lambda gi, dr: (dr, gi, 0, 0))
    y, hf_re, hf_im = pl.pallas_call(
        functools.partial(_s5_scan_kernel, bsz),
        grid=(g, 2),
        in_specs=[per_g(r, rows), per_g(1, rows), per_dg(rows, rows), per_dg(p, rows), per_dg(p, rows),
                  per_dg(rows, p), per_dg(rows, p), per_dg(2, p), per_dg(bsz, p), per_dg(bsz, p)],
        out_specs=[per_g(r, rows), per_dg(bsz, p), per_dg(bsz, p)],
        out_shape=[jax.ShapeDtypeStruct((g, r, rows), F32),
                   jax.ShapeDtypeStruct((2, g, bsz, p), F32),
                   jax.ShapeDtypeStruct((2, g, bsz, p), F32)],
        scratch_shapes=[pltpu.VMEM((r, p), F32)] * 4,
        compiler_params=_cparams("arbitrary", "arbitrary"),
        name="s5_scan",
    )(ug, dg, m, w_re, w_im, g_re, g_im, a, h0[0], h0[1])
    y = y.reshape(g, nc, bsz, t, i).transpose(2, 1, 3, 0, 4).reshape(bsz, n, width)
    return y, (hf_re, hf_im)


def _even_post_kernel(seg_len, alpha, ys_ref, pc_ref, h_ref, g1_ref, wout_ref, gluw_ref, glub_ref,
                      cw_ref, cb_ref, lng_ref, lnb_ref, o_ref):
    ys = ys_ref[0]
    tl, sw = ys.shape
    g = _gelu(ys)
    ya = g * jax.nn.sigmoid(_dot_bf16(g, gluw_ref[...]) + glub_ref[...])
    cwid = pc_ref.shape[-1] // 3
    z = pc_ref[0, :, cwid:2 * cwid] * pc_ref[0, :, 2 * cwid:]
    pos = lax.broadcasted_iota(jnp.int32, (tl, 1), 0) % seg_len
    z_prev = jnp.where(pos == 0, 0.0, pltpu.roll(z, 1, axis=0))
    z_next = jnp.where(pos == seg_len - 1, 0.0, pltpu.roll(z, tl - 1, axis=0))
    cw = cw_ref[...]
    conv = cw[0:1, :] * z_prev + cw[1:2, :] * z + cw[2:3, :] * z_next + cb_ref[...]
    yb = pc_ref[0, :, :cwid] * conv
    out = _dot_bf16(ya, wout_ref[:sw, :]) + _dot_bf16(yb, wout_ref[sw:, :])
    h = h_ref[0]
    o_ref[0] = _layer_norm(alpha * h + g1_ref[0] * out, lng_ref[...], lnb_ref[...])


def _even_post(y_ssm, p_conv, h, g1, w_out, glu_w, glu_b, conv_w, conv_b, ln_g, ln_b, seg_len, alpha):
    bsz, n, d = h.shape
    sw = y_ssm.shape[-1]
    cw3 = p_conv.shape[-1]
    cwid = cw3 // 3
    tl = _token_tile(n, 512)
    assert tl % seg_len == 0
    tok = lambda b, j: (b, j, 0)
    mod = lambda b, j: (b, 0, 0)
    full = lambda *shape: pl.BlockSpec(shape, lambda b, j: (0,) * len(shape))
    return pl.pallas_call(
        functools.partial(_even_post_kernel, seg_len, alpha),
        grid=(bsz, n // tl),
        in_specs=[pl.BlockSpec((1, tl, sw), tok), pl.BlockSpec((1, tl, cw3), tok),
                  pl.BlockSpec((1, tl, d), tok), pl.BlockSpec((1, 1, d), mod),
                  full(sw + cwid, d), full(sw, sw), full(1, sw), full(3, cwid), full(1, cwid),
                  full(1, d), full(1, d)],
        out_specs=pl.BlockSpec((1, tl, d), tok),
        out_shape=jax.ShapeDtypeStruct((bsz, n, d), F32),
        compiler_params=_cparams("arbitrary", "arbitrary"),
        name="even_post",
    )(y_ssm, p_conv, h, g1, w_out.astype(jnp.bfloat16), glu_w.astype(jnp.bfloat16), glu_b.reshape(1, sw), conv_w, conv_b.reshape(1, cwid),
      ln_g.reshape(1, d), ln_b.reshape(1, d))


def _pool_matrices(seg_len, tile):
    pos = np.arange(tile) % seg_len
    seg = np.arange(tile) // seg_len
    mats = []
    for w in POOL_WINDOWS:
        lo = np.clip(pos - w // 2, 0, seg_len - 1)
        hi = np.clip(pos + w // 2 - 1, 0, seg_len - 1)
        cnt = (hi - lo + 1).astype(np.float64)
        inside = ((seg[:, None] == seg[None, :]) & (pos[None, :] >= lo[:, None])
                  & (pos[None, :] <= hi[:, None]))
        mats.append(inside / cnt[:, None] - np.eye(tile))
    return np.stack(mats).astype(np.float32)


def _pool_kernel(alpha, h_ref, sc_ref, sh_ref, g1_ref, pm_ref, pw_ref, ps_ref, lng_ref, lnb_ref, o_ref):
    h = h_ref[0]
    a = h * (1.0 + sc_ref[0]) + sh_ref[0]
    n_win, pg, _ = pw_ref.shape
    ys = []
    for gi in range(n_win):
        z = _dot(pm_ref[gi], a[:, gi * pg:(gi + 1) * pg])
        ys.append(_dot_bf16(z, pw_ref[gi]))
    y = jnp.concatenate(ys, axis=-1) * ps_ref[...]
    o_ref[0] = _layer_norm(alpha * h + g1_ref[0] * y, lng_ref[...], lnb_ref[...])


def _pool_mixer(h, sc, sh, g1, pool_w, pool_scale, ln_g, ln_b, seg_len, alpha):
    bsz, n, d = h.shape
    tl = _token_tile(n, 256)
    assert tl % seg_len == 0
    pm = jnp.asarray(_pool_matrices(seg_len, tl))
    n_win, pg, _ = pool_w.shape
    tok = lambda b, j: (b, j, 0)
    mod = lambda b, j: (b, 0, 0)
    full = lambda *shape: pl.BlockSpec(shape, lambda b, j: (0,) * len(shape))
    return pl.pallas_call(
        functools.partial(_pool_kernel, alpha),
        grid=(bsz, n // tl),
        in_specs=[pl.BlockSpec((1, tl, d), tok), pl.BlockSpec((1, 1, d), mod), pl.BlockSpec((1, 1, d), mod),
                  pl.BlockSpec((1, 1, d), mod), full(n_win, tl, tl), full(n_win, pg, pg), full(1, d),
                  full(1, d), full(1, d)],
        out_specs=pl.BlockSpec((1, tl, d), tok),
        out_shape=jax.ShapeDtypeStruct((bsz, n, d), F32),
        compiler_params=_cparams("arbitrary", "arbitrary"),
        name="pool_mixer",
    )(h, sc, sh, g1, pm, pool_w.astype(jnp.bfloat16), pool_scale.reshape(1, d), ln_g.reshape(1, d), ln_b.reshape(1, d))


def _top_rows(s, n_top):
    n = s.shape[0]
    rid = lax.broadcasted_iota(jnp.int32, s.shape, 0).astype(F32)
    vals, rows = [], []
    for _ in range(n_top):
        m = jnp.max(s, axis=0, keepdims=True)
        sel = jnp.min(jnp.where(s == m, rid, float(n)), axis=0, keepdims=True)
        vals.append(m)
        rows.append(sel)
        s = jnp.where(rid == sel, -jnp.inf, s)
    return jnp.concatenate(vals, axis=0), jnp.concatenate(rows, axis=0)


def _candidate_slabs(k_top):
    slabs = []
    for a in range(k_top):
        n_valid = k_top // (a + 1)
        if n_valid == 1:
            slabs.append((a, k_top, 1, 1))
            break
        slabs.append((a, a + 1, -(-n_valid // 8) * 8, n_valid))
    return slabs


U_HALF_MASK = -65536
ROUTE_HEADS = 1
EXPERT_SLOTS = 8
PREFETCH_TOKENS = EXPERT_SLOTS - 1


def _expert_pitch(rows):
    tiles = -(-rows // V7X_SUBLANES)
    return V7X_SUBLANES * (tiles if tiles % 2 else tiles + 1)


def _peer_expert_kernel(alpha, pitch, idx_ref, gate_ref, h_ref, sc_ref, sh_ref, g2_ref, lng_ref, lnb_ref,
                        uv_hbm, o_ref, *scratch):
    slots, (xin_scr, f_scr, sems) = scratch[:EXPERT_SLOTS], scratch[EXPERT_SLOTS:]
    tb, n_sel = gate_ref.shape[1], gate_ref.shape[2]
    h = h_ref[0]
    d = h.shape[-1]
    lanes = slots[0].shape[-1]
    ru = d // lanes
    rows = ru
    per_stage = n_sel // (2 * ru)
    xin_scr[...] = h * (1.0 + sc_ref[0]) + sh_ref[0]
    step = pl.program_id(0) * pl.num_programs(1) + pl.program_id(1)
    n_steps = pl.num_programs(0) * pl.num_programs(1)

    def issue(t, slot, stage):
        for k in range(stage * per_stage, (stage + 1) * per_stage):
            e = idx_ref[t, k]
            src = uv_hbm.at[pl.ds(pl.multiple_of(e * rows, rows), rows), :]
            dst = slots[slot].at[pl.ds(k * pitch, rows), :]
            pltpu.make_async_copy(src, dst, sems.at[slot]).start(priority=k % 2)

    def wait(slot):
        total = n_sel * rows
        pltpu.make_async_copy(uv_hbm.at[pl.ds(0, total), :], slots[slot].at[pl.ds(0, total), :],
                              sems.at[slot]).wait()

    eye = (lax.broadcasted_iota(jnp.int32, (n_sel, n_sel), 0)
           == lax.broadcasted_iota(jnp.int32, (n_sel, n_sel), 1))

    @pl.when(step == 0)
    def _():
        for t0 in range(PREFETCH_TOKENS):
            for stage in range(2 * ru):
                issue(t0, t0 % EXPERT_SLOTS, stage)

    def token(t, slot):
        ahead = (slot + PREFETCH_TOKENS) % EXPERT_SLOTS
        buf = slots[slot]
        wait(slot)
        x = xin_scr[pl.ds(t, 1), :]
        dots = None
        for s in range(ru):
            issue(t + PREFETCH_TOKENS, ahead, s)
            words = buf[pl.ds(s, n_sel, stride=pitch), :]
            u_s = lax.bitcast_convert_type(words & U_HALF_MASK, F32)
            part = u_s * x[:, s * lanes:(s + 1) * lanes]
            dots = part if dots is None else dots + part
        g_row = gate_ref[0, pl.ds(t, 1), :]
        g_col = jnp.sum(jnp.where(eye, g_row, 0.0), axis=1, keepdims=True)
        coef = g_col * _gelu(jnp.sum(dots, axis=1, keepdims=True))
        f_parts = []
        for s in range(ru):
            issue(t + PREFETCH_TOKENS, ahead, ru + s)
            words = buf[pl.ds(s, n_sel, stride=pitch), :]
            v_s = lax.bitcast_convert_type(words << 16, F32)
            f_parts.append(jnp.sum(coef * v_s, axis=0, keepdims=True))
        f_scr[pl.ds(t, 1), :] = jnp.concatenate(f_parts, axis=1)

    def token_group(g, carry):
        for slot in range(EXPERT_SLOTS):
            token(g * EXPERT_SLOTS + slot, slot)
        return carry

    lax.fori_loop(0, tb // EXPERT_SLOTS, token_group, 0)

    @pl.when(step == n_steps - 1)
    def _():
        for t0 in range(PREFETCH_TOKENS):
            wait((tb + t0) % EXPERT_SLOTS)

    o_ref[0] = _layer_norm(alpha * h + g2_ref[0] * f_scr[...], lng_ref[...], lnb_ref[...])


def _expert_table(u_tab, v_tab):
    n_exp, d = u_tab.shape

    def bits(tab):
        return lax.bitcast_convert_type(tab.astype(jnp.bfloat16), jnp.uint16).astype(jnp.uint32)

    words = lax.bitcast_convert_type((bits(u_tab) << 16) | bits(v_tab), jnp.int32)
    return words.reshape(n_exp * d // V7X_LANES, V7X_LANES)


def _peer_experts(idx, gate, h, sc, sh, g2, ln_g, ln_b, uv_tab, alpha):
    bsz, n, d = h.shape
    n_sel = idx.shape[-1]
    rows = d // V7X_LANES
    assert d % V7X_LANES == 0 and n_sel % (2 * rows) == 0
    pitch = _expert_pitch(rows)
    tb = _token_tile(n, 64)
    assert tb % EXPERT_SLOTS == 0
    nb = n // tb
    idx_blk = idx.reshape(bsz * nb, tb, n_sel)
    first = idx_blk[:, :V7X_SUBLANES, :]
    halo = jnp.concatenate([first[1:], first[-1:]], axis=0)
    idx_rows = jnp.concatenate([idx_blk, halo], axis=1).reshape(bsz * nb * (tb + V7X_SUBLANES), n_sel)
    tok = lambda b, j: (b, j, 0)
    mod = lambda b, j: (b, 0, 0)
    full = lambda *shape: pl.BlockSpec(shape, lambda b, j: (0,) * len(shape))
    return pl.pallas_call(
        functools.partial(_peer_expert_kernel, alpha, pitch),
        grid=(bsz, nb),
        in_specs=[pl.BlockSpec((tb + V7X_SUBLANES, n_sel), lambda b, j: (b * nb + j, 0),
                               memory_space=pltpu.SMEM),
                  pl.BlockSpec((1, tb, n_sel), tok), pl.BlockSpec((1, tb, d), tok),
                  pl.BlockSpec((1, 1, d), mod), pl.BlockSpec((1, 1, d), mod), pl.BlockSpec((1, 1, d), mod),
                  full(1, d), full(1, d),
                  pl.BlockSpec(memory_space=pl.ANY)],
        out_specs=pl.BlockSpec((1, tb, d), tok),
        out_shape=jax.ShapeDtypeStruct((bsz, n, d), F32),
        scratch_shapes=[pltpu.VMEM((n_sel * pitch, V7X_LANES), jnp.int32)] * EXPERT_SLOTS
                       + [pltpu.VMEM((tb, d), F32), pltpu.VMEM((tb, d), F32),
                          pltpu.SemaphoreType.DMA((EXPERT_SLOTS,))],
        compiler_params=_cparams("arbitrary", "arbitrary"),
        name="peer_experts",
    )(idx_rows, gate, h, sc, sh, g2, ln_g.reshape(1, d), ln_b.reshape(1, d), uv_tab)


def _route_head(xin, wq_ref, keys_ref, head):
    n_keys = keys_ref.shape[1]
    k_top = PEER_TOPK
    tops = []
    for side in range(2):
        q = jnp.dot(xin, wq_ref[head * 2 + side], preferred_element_type=F32)
        tops.append(_top_rows(_dot_nt(keys_ref[side], q), k_top))
    (v0, i0), (v1, i1) = tops
    cs, ci = [], []
    for a_lo, a_hi, n_b, n_valid in _candidate_slabs(k_top):
        sl = v0[a_lo:a_hi] + v1[0:n_b]
        if n_valid < n_b:
            sl = jnp.where(lax.broadcasted_iota(jnp.int32, sl.shape, 0) < n_valid, sl, -jnp.inf)
        cs.append(sl)
        ci.append(i0[a_lo:a_hi] * float(n_keys) + i1[0:n_b])
    cand_s = jnp.concatenate(cs, axis=0)
    cand_i = jnp.concatenate(ci, axis=0)
    n_cand = cand_s.shape[0]
    rid = lax.broadcasted_iota(jnp.int32, cand_s.shape, 0).astype(F32)
    best_s, best_i = [], []
    s = cand_s
    for _ in range(k_top):
        m = jnp.max(s, axis=0, keepdims=True)
        sel = jnp.min(jnp.where(s == m, rid, float(n_cand)), axis=0, keepdims=True)
        hit = rid == sel
        best_s.append(m)
        best_i.append(jnp.sum(jnp.where(hit, cand_i, 0.0), axis=0, keepdims=True))
        s = jnp.where(hit, -jnp.inf, s)
    bs = jnp.concatenate(best_s, axis=0)
    e = jnp.exp(bs - bs[0:1, :])
    gate = e / jnp.sum(e, axis=0, keepdims=True)
    return jnp.concatenate(best_i, axis=0).astype(jnp.int32), gate


def _peer_ffn_kernel(alpha, pitch, n_heads, h_ref, hn_ref, sc_ref, sh_ref, scn_ref, shn_ref, g2_ref,
                     lng_ref, lnb_ref, wq_ref, keys_ref, uv_hbm, o_ref, *scratch):
    slots = scratch[:EXPERT_SLOTS]
    (xin_scr, xnext_scr, f_scr, idxn_scr, gaten_scr, idxt_scr, gate_scr, idx_smem, sems,
     pub_sem) = scratch[EXPERT_SLOTS:]
    tb, d = h_ref.shape[1], h_ref.shape[2]
    n_sel = n_heads * PEER_TOPK
    lanes = slots[0].shape[-1]
    ru = d // lanes
    per_stage = n_sel // (2 * ru)
    step = pl.program_id(0) * pl.num_programs(1) + pl.program_id(1)
    n_steps = pl.num_programs(0) * pl.num_programs(1)
    cur = step % 2
    nxt = 1 - cur

    def route_into(xin, head):
        idx, gate = _route_head(xin, wq_ref, keys_ref, head)
        rows = pl.ds(pl.multiple_of(head * PEER_TOPK, PEER_TOPK), PEER_TOPK)
        idxn_scr[rows, :] = idx
        gaten_scr[rows, :] = gate

    def publish(buf):
        gate_scr[buf] = gaten_scr[...].T
        idxt_scr[...] = idxn_scr[...].astype(F32).T.astype(jnp.int32)
        rows = pl.ds(pl.multiple_of(buf * tb, tb), tb)
        copy = pltpu.make_async_copy(idxt_scr, idx_smem.at[rows, :], pub_sem.at[0])
        copy.start()
        copy.wait()

    def index_row(tt):
        return jnp.where(tt >= tb, nxt * tb + tt - tb, cur * tb + tt)

    def issue(row, slot, stage):
        for k in range(stage * per_stage, (stage + 1) * per_stage):
            e = idx_smem[row, k]
            src = uv_hbm.at[pl.ds(pl.multiple_of(e * ru, ru), ru), :]
            dst = slots[slot].at[pl.ds(k * pitch, ru), :]
            pltpu.make_async_copy(src, dst, sems.at[slot]).start(priority=k % 2)

    def wait(slot):
        total = n_sel * ru
        pltpu.make_async_copy(uv_hbm.at[pl.ds(0, total), :], slots[slot].at[pl.ds(0, total), :],
                              sems.at[slot]).wait()

    h = h_ref[0]
    xin_scr[...] = h * (1.0 + sc_ref[0]) + sh_ref[0]
    xnext_scr[...] = (hn_ref[0] * (1.0 + scn_ref[0]) + shn_ref[0]).astype(jnp.bfloat16)

    @pl.when(step == 0)
    def _():
        xin = xin_scr[...].astype(jnp.bfloat16)

        def first_route(head, carry):
            route_into(xin, head)
            return carry

        lax.fori_loop(0, n_heads, first_route, 0)
        publish(0)
        for t0 in range(PREFETCH_TOKENS):
            for stage in range(2 * ru):
                issue(t0, t0 % EXPERT_SLOTS, stage)

    eye = (lax.broadcasted_iota(jnp.int32, (n_sel, n_sel), 0)
           == lax.broadcasted_iota(jnp.int32, (n_sel, n_sel), 1))

    def token(t, slot):
        ahead = (slot + PREFETCH_TOKENS) % EXPERT_SLOTS
        ahead_row = index_row(t + PREFETCH_TOKENS)
        buf = slots[slot]
        wait(slot)
        x = xin_scr[pl.ds(t, 1), :]
        dots = None
        for s in range(ru):
            issue(ahead_row, ahead, s)
            words = buf[pl.ds(s, n_sel, stride=pitch), :]
            u_s = lax.bitcast_convert_type(words & U_HALF_MASK, F32)
            part = u_s * x[:, s * lanes:(s + 1) * lanes]
            dots = part if dots is None else dots + part
        g_row = gate_scr[cur, pl.ds(t, 1), :]
        g_col = jnp.sum(jnp.where(eye, g_row, 0.0), axis=1, keepdims=True)
        coef = g_col * _gelu(jnp.sum(dots, axis=1, keepdims=True))
        f_parts = []
        for s in range(ru):
            issue(ahead_row, ahead, ru + s)
            words = buf[pl.ds(s, n_sel, stride=pitch), :]
            v_s = lax.bitcast_convert_type(words << 16, F32)
            f_parts.append(jnp.sum(coef * v_s, axis=0, keepdims=True))
        f_scr[pl.ds(t, 1), :] = jnp.concatenate(f_parts, axis=1)

    def token_group(g, carry):
        for slot in range(EXPERT_SLOTS):
            token(g * EXPERT_SLOTS + slot, slot)
        return carry

    groups = tb // EXPERT_SLOTS
    n_phases = n_heads // ROUTE_HEADS
    groups_per_phase = (groups * 3 // 4) // n_phases

    def route_phase(phase, carry):
        lax.fori_loop(0, groups_per_phase, lambda g, c: token_group(phase * groups_per_phase + g, c), 0)
        xin = xnext_scr[...]
        for r in range(ROUTE_HEADS):
            route_into(xin, phase * ROUTE_HEADS + r)
        return carry

    lax.fori_loop(0, n_phases, route_phase, 0)
    publish(nxt)
    lax.fori_loop(n_phases * groups_per_phase, groups, token_group, 0)

    @pl.when(step == n_steps - 1)
    def _():
        for t0 in range(PREFETCH_TOKENS):
            wait((tb + t0) % EXPERT_SLOTS)

    o_ref[0] = _layer_norm(alpha * h + g2_ref[0] * f_scr[...], lng_ref[...], lnb_ref[...])


def _peer_block(h, sc2, sh2, g2, wq, keys, uv_tab, ln_g, ln_b, alpha):
    bsz, n, d = h.shape
    _, n_keys, half = keys.shape
    n_heads = wq.shape[1] // (2 * half)
    n_sel = n_heads * PEER_TOPK
    ru = d // V7X_LANES
    assert d % V7X_LANES == 0 and n_sel % (2 * ru) == 0
    pitch = _expert_pitch(ru)
    tb = _token_tile(n, 256)
    groups = tb // EXPERT_SLOTS
    n_phases = n_heads // ROUTE_HEADS
    assert tb % EXPERT_SLOTS == 0 and n_heads % ROUTE_HEADS == 0 and (groups * 3 // 4) // n_phases >= 1
    assert (groups - n_phases * ((groups * 3 // 4) // n_phases)) * EXPERT_SLOTS >= PREFETCH_TOKENS
    nb = n // tb
    last = bsz * nb - 1
    wq_heads = wq.astype(jnp.bfloat16).reshape(d, 2 * n_heads, half).transpose(1, 0, 2)

    def next_block(b, j):
        flat = jnp.minimum(b * nb + j + 1, last)
        return flat // nb, flat % nb

    tok = lambda b, j: (b, j, 0)
    tok_next = lambda b, j: next_block(b, j) + (0,)
    mod = lambda b, j: (b, 0, 0)
    mod_next = lambda b, j: (next_block(b, j)[0], 0, 0)
    full = lambda *shape: pl.BlockSpec(shape, lambda b, j: (0,) * len(shape))
    return pl.pallas_call(
        functools.partial(_peer_ffn_kernel, alpha, pitch, n_heads),
        grid=(bsz, nb),
        in_specs=[pl.BlockSpec((1, tb, d), tok), pl.BlockSpec((1, tb, d), tok_next),
                  pl.BlockSpec((1, 1, d), mod), pl.BlockSpec((1, 1, d), mod),
                  pl.BlockSpec((1, 1, d), mod_next), pl.BlockSpec((1, 1, d), mod_next),
                  pl.BlockSpec((1, 1, d), mod), full(1, d), full(1, d),
                  full(2 * n_heads, d, half), full(2, n_keys, half),
                  pl.BlockSpec(memory_space=pl.ANY)],
        out_specs=pl.BlockSpec((1, tb, d), tok),
        out_shape=jax.ShapeDtypeStruct((bsz, n, d), F32),
        scratch_shapes=[pltpu.VMEM((n_sel * pitch, V7X_LANES), jnp.int32)] * EXPERT_SLOTS
                       + [pltpu.VMEM((tb, d), F32), pltpu.VMEM((tb, d), jnp.bfloat16), pltpu.VMEM((tb, d), F32),
                          pltpu.VMEM((n_sel, tb), jnp.int32), pltpu.VMEM((n_sel, tb), F32),
                          pltpu.VMEM((tb, n_sel), jnp.int32),
                          pltpu.VMEM((2, tb, n_sel), F32), pltpu.SMEM((2 * tb, n_sel), jnp.int32),
                          pltpu.SemaphoreType.DMA((EXPERT_SLOTS,)), pltpu.SemaphoreType.DMA((1,))],
        compiler_params=_cparams("arbitrary", "arbitrary"),
        name="peer_ffn",
    )(h, h, sc2, sh2, sc2, sh2, g2, ln_g.reshape(1, d), ln_b.reshape(1, d), wq_heads, keys, uv_tab)


def kernel(x, c, ctx, c_ctx, ada_w, ada_b, ln_mix_g, ln_mix_b, ln_ffn_g, ln_ffn_b, even_w_in, even_w_out, ssm_lam_re, ssm_lam_im, ssm_log_step, ssm_b_re, ssm_b_im, ssm_c_re, ssm_c_im, ssm_d, glu_w, glu_b, conv_w, conv_b, pool_w, pool_scale, peer_wq, peer_keys, peer_u, peer_v):
    depth, d, _ = ada_w.shape
    bsz, seq, _ = x.shape
    ctx_len = ctx.shape[1]
    ssm_w = ssm_d.shape[-1]
    alpha = (2.0 * depth) ** 0.25

    pad = (-(bsz + 1)) % 8
    c_all = jnp.concatenate([c, c_ctx[None, :], jnp.zeros((pad, d), F32)], axis=0)
    mod = _ada_modulation(c_all, ada_w, ada_b)

    def six(rows):
        return [rows[:, None, k * d:(k + 1) * d] for k in range(6)]

    h_lat, h_ctx = x, ctx
    for layer in range(depth):
        ctx_out = any(j > layer and j % 2 == 0 for j in range(depth))
        even = layer % 2 == 0
        sh1, sc1, g1, sh2, sc2, g2 = six(mod[layer, :bsz])
        need_ctx = ctx_out or even
        if need_ctx:
            csh1, csc1, cg1, csh2, csc2, cg2 = six(jnp.broadcast_to(mod[layer, bsz:bsz + 1], (bsz, 6 * d)))
        lmg, lmb = ln_mix_g[layer], ln_mix_b[layer]
        if even:
            e = layer // 2
            ops = _s5_operators(ssm_lam_re[e], ssm_lam_im[e], ssm_log_step[e], ssm_b_re[e], ssm_b_im[e],
                                ssm_c_re[e], ssm_c_im[e])
            post = functools.partial(_even_post, w_out=even_w_out[e], glu_w=glu_w[e], glu_b=glu_b[e],
                                     conv_w=conv_w[e], conv_b=conv_b[e], ln_g=lmg, ln_b=lmb, alpha=alpha)
            ps_ctx, pc_ctx = _in_projection(h_ctx, csc1, csh1, even_w_in[e], ssm_w)
            y_ctx, finals = _s5_bidir(ps_ctx, ops, ssm_d[e], None)
            ps_lat, pc_lat = _in_projection(h_lat, sc1, sh1, even_w_in[e], ssm_w)
            y_lat, _ = _s5_bidir(ps_lat, ops, ssm_d[e], finals)
            h_lat = post(y_lat, pc_lat, h_lat, g1, seg_len=GRID_W)
            if ctx_out:
                h_ctx = post(y_ctx, pc_ctx, h_ctx, cg1, seg_len=ctx_len)
        else:
            o = layer // 2
            h_lat = _pool_mixer(h_lat, sc1, sh1, g1, pool_w[o], pool_scale[o], lmg, lmb, GRID_W, alpha)
            if ctx_out:
                h_ctx = _pool_mixer(h_ctx, csc1, csh1, cg1, pool_w[o], pool_scale[o], lmg, lmb, ctx_len, alpha)
        uv_tab = _expert_table(peer_u[layer], peer_v[layer])
        peer = functools.partial(_peer_block, wq=peer_wq[layer], keys=peer_keys[layer], uv_tab=uv_tab,
                                 ln_g=ln_ffn_g[layer], ln_b=ln_ffn_b[layer], alpha=alpha)
        h_lat = peer(h_lat, sc2, sh2, g2)
        if ctx_out:
            h_ctx = peer(h_ctx, csc2, csh2, cg2)
    return h_lat
```

```python
import functools
import math

import numpy as np
import jax
import jax.numpy as jnp
from jax import lax
from jax.experimental import pallas as pl
from jax.experimental.pallas import tpu as pltpu

F32 = jnp.float32
HI = lax.Precision.HIGHEST

GRID_W = 64
POOL_WINDOWS = (2, 4, 8, 16)
PEER_TOPK = 16
LN_EPS = 1e-5
STEP_CHUNK = 64
VMEM_LIMIT_V7X = 56 * 1024 * 1024
V7X_SUBLANES, V7X_LANES = 8, 128
NT_DIMS = (((1,), (1,)), ((), ()))


def _cparams(*sem):
    return pltpu.CompilerParams(dimension_semantics=sem, vmem_limit_bytes=VMEM_LIMIT_V7X)


def _dot(a, b):
    return jnp.dot(a, b, preferred_element_type=F32, precision=HI)


def _dot_bf16(a, w):
    return jnp.dot(a.astype(jnp.bfloat16), w, preferred_element_type=F32)


def _dot_3pass(a, b):
    def split(x):
        hi = x.astype(jnp.bfloat16)
        return hi, (x - hi.astype(F32)).astype(jnp.bfloat16)

    a_hi, a_lo = split(a)
    b_hi, b_lo = split(b)
    dot = functools.partial(jnp.dot, preferred_element_type=F32)
    return dot(a_hi, b_hi) + (dot(a_hi, b_lo) + dot(a_lo, b_hi))


def _dot_nt(a, b):
    return lax.dot_general(a, b, NT_DIMS, preferred_element_type=F32, precision=HI)


def _gelu(x):
    return 0.5 * x * (1.0 + lax.erf(x * (1.0 / math.sqrt(2.0))))


def _layer_norm(x, g, b):
    mu = jnp.mean(x, axis=-1, keepdims=True)
    xc = x - mu
    var = jnp.mean(xc * xc, axis=-1, keepdims=True)
    return xc * lax.rsqrt(var + LN_EPS) * g + b


def _token_tile(n, cap):
    t = min(n, cap)
    assert n % t == 0
    return t


def _ada_kernel(c_ref, w_ref, b_ref, o_ref):
    c = c_ref[...]
    s = c * jax.nn.sigmoid(c)
    o_ref[0] = _dot(s, w_ref[0]) + b_ref[0]


def _ada_modulation(c_all, ada_w, ada_b):
    depth, d, e = ada_w.shape
    rows = c_all.shape[0]
    te = e // 6
    return pl.pallas_call(
        _ada_kernel,
        grid=(depth, e // te),
        in_specs=[pl.BlockSpec((rows, d), lambda l, j: (0, 0)),
                  pl.BlockSpec((1, d, te), lambda l, j: (l, 0, j)),
                  pl.BlockSpec((1, 1, te), lambda l, j: (l, 0, j))],
        out_specs=pl.BlockSpec((1, rows, te), lambda l, j: (l, 0, j)),
        out_shape=jax.ShapeDtypeStruct((depth, rows, e), F32),
        compiler_params=_cparams("arbitrary", "arbitrary"),
        name="ada_modulation",
    )(c_all, ada_w, ada_b.reshape(depth, 1, e))


def _inproj_kernel(ssm_w, h_ref, sc_ref, sh_ref, w_ref, ps_ref, pc_ref):
    a = h_ref[0] * (1.0 + sc_ref[0]) + sh_ref[0]
    a = a.astype(jnp.bfloat16)
    ps_ref[0] = _dot_bf16(a, w_ref[:, :ssm_w])
    pc_ref[0] = _dot_bf16(a, w_ref[:, ssm_w:])


def _in_projection(h, sc, sh, w_in, ssm_w):
    bsz, n, d = h.shape
    e = w_in.shape[1]
    tl = _token_tile(n, 512)
    tok = lambda b, j: (b, j, 0)
    mod = lambda b, j: (b, 0, 0)
    return pl.pallas_call(
        functools.partial(_inproj_kernel, ssm_w),
        grid=(bsz, n // tl),
        in_specs=[pl.BlockSpec((1, tl, d), tok),
                  pl.BlockSpec((1, 1, d), mod),
                  pl.BlockSpec((1, 1, d), mod),
                  pl.BlockSpec((d, e), lambda b, j: (0, 0))],
        out_specs=[pl.BlockSpec((1, tl, ssm_w), tok),
                   pl.BlockSpec((1, tl, e - ssm_w), tok)],
        out_shape=[jax.ShapeDtypeStruct((bsz, n, ssm_w), F32),
                   jax.ShapeDtypeStruct((bsz, n, e - ssm_w), F32)],
        compiler_params=_cparams("arbitrary", "arbitrary"),
        name="even_in_projection",
    )(h, sc, sh, w_in.astype(jnp.bfloat16))


def _s5_param_kernel(chunk, lam_row_ref, lam_col_ref, ls_ref, bt_re_ref, bt_im_ref,
                     ct_re_ref, ct_im_ref, k_ref, w_re_ref, w_im_ref, g_re_ref, g_im_ref, a_ref):
    backward = pl.program_id(0) == 1
    p_dim = lam_row_ref.shape[-1]
    i_dim = k_ref.shape[-1]
    rows = chunk * i_dim
    step = jnp.exp(ls_ref[0, 0])
    lam_row = lam_row_ref[0, 0]
    lr_row, li_row = lam_row[0:1, :], lam_row[1:2, :]
    lam_col = lam_col_ref[0, 0]
    lr_col, li_col = lam_col[:, 0:1], lam_col[:, 1:2]
    ldr_row, ldi_row = lr_row * step, li_row * step
    ldr_col, ldi_col = lr_col * step, li_col * step

    def power_rows(expo):
        mag = jnp.exp(expo * ldr_row)
        return mag * jnp.cos(expo * ldi_row), mag * jnp.sin(expo * ldi_row)

    mag = jnp.exp(ldr_col)
    ab_re, ab_im = mag * jnp.cos(ldi_col), mag * jnp.sin(ldi_col)
    den = lr_col * lr_col + li_col * li_col
    nr = ab_re - 1.0
    f_re = (nr * lr_col + ab_im * li_col) / den
    f_im = (ab_im * lr_col - nr * li_col) / den
    bt_re, bt_im = bt_re_ref[0, 0], bt_im_ref[0, 0]
    bbt_re = f_re * bt_re - f_im * bt_im
    bbt_im = f_re * bt_im + f_im * bt_re
    bb_re, bb_im = bbt_re[:, :i_dim], bbt_im[:, :i_dim]

    ct_re, ct_im = ct_re_ref[0, 0], ct_im_ref[0, 0]
    tok = (lax.broadcasted_iota(jnp.int32, (rows, 1), 0) // i_dim).astype(F32)

    e_re, e_im = power_rows(tok)
    ck_re = ct_re * e_re - ct_im * e_im
    ck_im = ct_re * e_im + ct_im * e_re
    k_ref[0, 0] = _dot(ck_re, bb_re) - _dot(ck_im, bb_im)

    f_exp = jnp.where(backward, float(chunk) - tok, tok + 1.0)
    e_re, e_im = power_rows(f_exp)
    g_re_ref[0, 0] = ct_re * e_re - ct_im * e_im
    g_im_ref[0, 0] = ct_re * e_im + ct_im * e_re

    tok_l = (lax.broadcasted_iota(jnp.int32, (1, rows), 1) // i_dim).astype(F32)
    e_exp = jnp.where(backward, tok_l, float(chunk - 1) - tok_l)
    magw = jnp.exp(ldr_col * e_exp)
    ew_re, ew_im = magw * jnp.cos(ldi_col * e_exp), magw * jnp.sin(ldi_col * e_exp)
    w_re_ref[0, 0] = ew_re * bbt_re - ew_im * bbt_im
    w_im_ref[0, 0] = ew_re * bbt_im + ew_im * bbt_re

    full = jnp.full((1, 1), float(chunk), F32)
    a_re, a_im = power_rows(full)
    a_ref[0, 0] = jnp.concatenate([a_re, a_im], axis=0)


def _s5_operators(lam_re, lam_im, log_step, b_re, b_im, c_re, c_im):
    _, g, p = lam_re.shape
    i = b_re.shape[-1]
    t = STEP_CHUNK
    rows = t * i
    lam_row = jnp.stack([lam_re, lam_im], axis=2)
    lam_col = jnp.stack([lam_re, lam_im], axis=3)
    ls = log_step.reshape(2, g, 1, 1)
    bt_re = jnp.tile(b_re, (1, 1, 1, t))
    bt_im = jnp.tile(b_im, (1, 1, 1, t))
    ct_re = jnp.tile(c_re, (1, 1, t, 1))
    ct_im = jnp.tile(c_im, (1, 1, t, 1))
    blk = lambda *shape: pl.BlockSpec((1, 1) + shape, lambda dr, gi: (dr, gi, 0, 0))
    k, w_re, w_im, g_re, g_im, a = pl.pallas_call(
        functools.partial(_s5_param_kernel, t),
        grid=(2, g),
        in_specs=[blk(2, p), blk(p, 2), blk(1, 1), blk(p, rows), blk(p, rows),
                  blk(rows, p), blk(rows, p)],
        out_specs=[blk(rows, i), blk(p, rows), blk(p, rows), blk(rows, p), blk(rows, p), blk(2, p)],
        out_shape=[jax.ShapeDtypeStruct((2, g, rows, i), F32),
                   jax.ShapeDtypeStruct((2, g, p, rows), F32),
                   jax.ShapeDtypeStruct((2, g, p, rows), F32),
                   jax.ShapeDtypeStruct((2, g, rows, p), F32),
                   jax.ShapeDtypeStruct((2, g, rows, p), F32),
                   jax.ShapeDtypeStruct((2, g, 2, p), F32)],
        compiler_params=_cparams("arbitrary", "arbitrary"),
        name="s5_operators",
    )(lam_row, lam_col, ls, bt_re, bt_im, ct_re, ct_im)

    k5 = k.reshape(2, g, t, i, i)
    s_idx = np.arange(t)[:, None]
    t_idx = np.arange(t)[None, :]

    def toeplitz(kd, lag):
        m = kd[:, np.clip(lag, 0, t - 1)]
        m = jnp.where((lag >= 0)[None, :, :, None, None], m, 0.0)
        return m.transpose(0, 1, 4, 2, 3).reshape(g, rows, rows)

    m = jnp.stack([toeplitz(k5[0], t_idx - s_idx), toeplitz(k5[1], s_idx - t_idx)])
    return m, w_re, w_im, g_re, g_im, a


def _s5_scan_kernel(bsz, u_ref, d_ref, m_ref, w_re_ref, w_im_ref, g_re_ref, g_im_ref, a_ref,
                    h0_re_ref, h0_im_ref, y_ref, hf_re_ref, hf_im_ref,
                    z_re_scr, z_im_scr, hp_re_scr, hp_im_scr):
    backward = pl.program_id(1) == 1
    u = u_ref[0]
    n_chunks = u.shape[0] // bsz
    z_re_scr[...] = _dot_nt(u, w_re_ref[0, 0])
    z_im_scr[...] = _dot_nt(u, w_im_ref[0, 0])
    a = a_ref[0, 0]
    a_re, a_im = a[0:1, :], a[1:2, :]

    def chunk_step(k, carry):
        h_re, h_im = carry
        c = jnp.where(backward, n_chunks - 1 - k, k)
        rows = pl.ds(pl.multiple_of(c * bsz, bsz), bsz)
        hp_re_scr[rows, :] = h_re
        hp_im_scr[rows, :] = h_im
        n_re = a_re * h_re - a_im * h_im + z_re_scr[rows, :]
        n_im = a_re * h_im + a_im * h_re + z_im_scr[rows, :]
        return n_re, n_im

    h_re, h_im = lax.fori_loop(0, n_chunks, chunk_step, (h0_re_ref[0, 0], h0_im_ref[0, 0]))
    hf_re_ref[0, 0] = h_re
    hf_im_ref[0, 0] = h_im

    y = (_dot_3pass(u, m_ref[0, 0]) + _dot_nt(hp_re_scr[...], g_re_ref[0, 0])
         - _dot_nt(hp_im_scr[...], g_im_ref[0, 0]))

    @pl.when(jnp.logical_not(backward))
    def _():
        y_ref[0] = d_ref[0] * u + y

    @pl.when(backward)
    def _():
        y_ref[0] = y_ref[0] + y


def _s5_bidir(u, ops, d_skip, h0):
    m, w_re, w_im, g_re, g_im, a = ops
    bsz, n, width = u.shape
    _, g, p, rows = w_re.shape
    t = STEP_CHUNK
    i = rows // t
    nc = n // t
    r = nc * bsz
    ug = u.reshape(bsz, nc, t, g, i).transpose(3, 1, 0, 2, 4).reshape(g, r, rows)
    dg = jnp.tile(d_skip.reshape(g, 1, i), (1, 1, t))
    if h0 is None:
        h0 = (jnp.zeros((2, g, bsz, p), F32),) * 2
    per_g = lambda *shape: pl.BlockSpec((1,) + shape, lambda gi, dr: (gi, 0, 0))
    per_dg = lambda *shape: pl.BlockSpec((1, 1) + shape, lambda gi, dr: (dr, gi, 0, 0))
    y, hf_re, hf_im = pl.pallas_call(
        functools.partial(_s5_scan_kernel, bsz),
        grid=(g, 2),
        in_specs=[per_g(r, rows), per_g(1, rows), per_dg(rows, rows), per_dg(p, rows), per_dg(p, rows),
                  per_dg(rows, p), per_dg(rows, p), per_dg(2, p), per_dg(bsz, p), per_dg(bsz, p)],
        out_specs=[per_g(r, rows), per_dg(bsz, p), per_dg(bsz, p)],
        out_shape=[jax.ShapeDtypeStruct((g, r, rows), F32),
                   jax.ShapeDtypeStruct((2, g, bsz, p), F32),
                   jax.ShapeDtypeStruct((2, g, bsz, p), F32)],
        scratch_shapes=[pltpu.VMEM((r, p), F32)] * 4,
        compiler_params=_cparams("arbitrary", "arbitrary"),
        name="s5_scan",
    )(ug, dg, m, w_re, w_im, g_re, g_im, a, h0[0], h0[1])
    y = y.reshape(g, nc, bsz, t, i).transpose(2, 1, 3, 0, 4).reshape(bsz, n, width)
    return y, (hf_re, hf_im)


def _even_post_kernel(seg_len, alpha, ys_ref, pc_ref, h_ref, g1_ref, wout_ref, gluw_ref, glub_ref,
                      cw_ref, cb_ref, lng_ref, lnb_ref, o_ref):
    ys = ys_ref[0]
    tl, sw = ys.shape
    g = _gelu(ys)
    ya = g * jax.nn.sigmoid(_dot_bf16(g, gluw_ref[...]) + glub_ref[...])
    cwid = pc_ref.shape[-1] // 3
    z = pc_ref[0, :, cwid:2 * cwid] * pc_ref[0, :, 2 * cwid:]
    pos = lax.broadcasted_iota(jnp.int32, (tl, 1), 0) % seg_len
    z_prev = jnp.where(pos == 0, 0.0, pltpu.roll(z, 1, axis=0))
    z_next = jnp.where(pos == seg_len - 1, 0.0, pltpu.roll(z, tl - 1, axis=0))
    cw = cw_ref[...]
    conv = cw[0:1, :] * z_prev + cw[1:2, :] * z + cw[2:3, :] * z_next + cb_ref[...]
    yb = pc_ref[0, :, :cwid] * conv
    out = _dot_bf16(ya, wout_ref[:sw, :]) + _dot_bf16(yb, wout_ref[sw:, :])
    h = h_ref[0]
    o_ref[0] = _layer_norm(alpha * h + g1_ref[0] * out, lng_ref[...], lnb_ref[...])


def _even_post(y_ssm, p_conv, h, g1, w_out, glu_w, glu_b, conv_w, conv_b, ln_g, ln_b, seg_len, alpha):
    bsz, n, d = h.shape
    sw = y_ssm.shape[-1]
    cw3 = p_conv.shape[-1]
    cwid = cw3 // 3
    tl = _token_tile(n, 512)
    assert tl % seg_len == 0
    tok = lambda b, j: (b, j, 0)
    mod = lambda b, j: (b, 0, 0)
    full = lambda *shape: pl.BlockSpec(shape, lambda b, j: (0,) * len(shape))
    return pl.pallas_call(
        functools.partial(_even_post_kernel, seg_len, alpha),
        grid=(bsz, n // tl),
        in_specs=[pl.BlockSpec((1, tl, sw), tok), pl.BlockSpec((1, tl, cw3), tok),
                  pl.BlockSpec((1, tl, d), tok), pl.BlockSpec((1, 1, d), mod),
                  full(sw + cwid, d), full(sw, sw), full(1, sw), full(3, cwid), full(1, cwid),
                  full(1, d), full(1, d)],
        out_specs=pl.BlockSpec((1, tl, d), tok),
        out_shape=jax.ShapeDtypeStruct((bsz, n, d), F32),
        compiler_params=_cparams("arbitrary", "arbitrary"),
        name="even_post",
    )(y_ssm, p_conv, h, g1, w_out.astype(jnp.bfloat16), glu_w.astype(jnp.bfloat16), glu_b.reshape(1, sw), conv_w, conv_b.reshape(1, cwid),
      ln_g.reshape(1, d), ln_b.reshape(1, d))


def _pool_matrices(seg_len, tile):
    pos = np.arange(tile) % seg_len
    seg = np.arange(tile) // seg_len
    mats = []
    for w in POOL_WINDOWS:
        lo = np.clip(pos - w // 2, 0, seg_len - 1)
        hi = np.clip(pos + w // 2 - 1, 0, seg_len - 1)
        cnt = (hi - lo + 1).astype(np.float64)
        inside = ((seg[:, None] == seg[None, :]) & (pos[None, :] >= lo[:, None])
                  & (pos[None, :] <= hi[:, None]))
        mats.append(inside / cnt[:, None] - np.eye(tile))
    return np.stack(mats).astype(np.float32)


def _pool_kernel(alpha, h_ref, sc_ref, sh_ref, g1_ref, pm_ref, pw_ref, ps_ref, lng_ref, lnb_ref, o_ref):
    h = h_ref[0]
    a = h * (1.0 + sc_ref[0]) + sh_ref[0]
    n_win, pg, _ = pw_ref.shape
    ys = []
    for gi in range(n_win):
        z = _dot(pm_ref[gi], a[:, gi * pg:(gi + 1) * pg])
        ys.append(_dot_bf16(z, pw_ref[gi]))
    y = jnp.concatenate(ys, axis=-1) * ps_ref[...]
    o_ref[0] = _layer_norm(alpha * h + g1_ref[0] * y, lng_ref[...], lnb_ref[...])


def _pool_mixer(h, sc, sh, g1, pool_w, pool_scale, ln_g, ln_b, seg_len, alpha):
    bsz, n, d = h.shape
    tl = _token_tile(n, 256)
    assert tl % seg_len == 0
    pm = jnp.asarray(_pool_matrices(seg_len, tl))
    n_win, pg, _ = pool_w.shape
    tok = lambda b, j: (b, j, 0)
    mod = lambda b, j: (b, 0, 0)
    full = lambda *shape: pl.BlockSpec(shape, lambda b, j: (0,) * len(shape))
    return pl.pallas_call(
        functools.partial(_pool_kernel, alpha),
        grid=(bsz, n // tl),
        in_specs=[pl.BlockSpec((1, tl, d), tok), pl.BlockSpec((1, 1, d), mod), pl.BlockSpec((1, 1, d), mod),
                  pl.BlockSpec((1, 1, d), mod), full(n_win, tl, tl), full(n_win, pg, pg), full(1, d),
                  full(1, d), full(1, d)],
        out_specs=pl.BlockSpec((1, tl, d), tok),
        out_shape=jax.ShapeDtypeStruct((bsz, n, d), F32),
        compiler_params=_cparams("arbitrary", "arbitrary"),
        name="pool_mixer",
    )(h, sc, sh, g1, pm, pool_w.astype(jnp.bfloat16), pool_scale.reshape(1, d), ln_g.reshape(1, d), ln_b.reshape(1, d))


def _top_rows(s, n_top):
    n = s.shape[0]
    rid = lax.broadcasted_iota(jnp.int32, s.shape, 0).astype(F32)
    vals, rows = [], []
    for _ in range(n_top):
        m = jnp.max(s, axis=0, keepdims=True)
        sel = jnp.min(jnp.where(s == m, rid, float(n)), axis=0, keepdims=True)
        vals.append(m)
        rows.append(sel)
        s = jnp.where(rid == sel, -jnp.inf, s)
    return jnp.concatenate(vals, axis=0), jnp.concatenate(rows, axis=0)


def _candidate_slabs(k_top):
    slabs = []
    for a in range(k_top):
        n_valid = k_top // (a + 1)
        if n_valid == 1:
            slabs.append((a, k_top, 1, 1))
            break
        slabs.append((a, a + 1, -(-n_valid // 8) * 8, n_valid))
    return slabs


U_HALF_MASK = -65536
ROUTE_HEADS = 1
EXPERT_SLOTS = 8
PREFETCH_TOKENS = EXPERT_SLOTS - 1


def _expert_pitch(rows):
    tiles = -(-rows // V7X_SUBLANES)
    return V7X_SUBLANES * (tiles if tiles % 2 else tiles + 1)


def _expert_table(u_tab, v_tab):
    n_exp, d = u_tab.shape

    def bits(tab):
        return lax.bitcast_convert_type(tab.astype(jnp.bfloat16), jnp.uint16).astype(jnp.uint32)

    words = lax.bitcast_convert_type((bits(u_tab) << 16) | bits(v_tab), jnp.int32)
    return words.reshape(n_exp * d // V7X_LANES, V7X_LANES)


def _route_head(xin, wq_ref, keys_ref, head):
    n_keys = keys_ref.shape[1]
    k_top = PEER_TOPK
    tops = []
    for side in range(2):
        q = jnp.dot(xin, wq_ref[head * 2 + side], preferred_element_type=F32)
        tops.append(_top_rows(_dot_nt(keys_ref[side], q), k_top))
    (v0, i0), (v1, i1) = tops
    cs, ci = [], []
    for a_lo, a_hi, n_b, n_valid in _candidate_slabs(k_top):
        sl = v0[a_lo:a_hi] + v1[0:n_b]
        if n_valid < n_b:
            sl = jnp.where(lax.broadcasted_iota(jnp.int32, sl.shape, 0) < n_valid, sl, -jnp.inf)
        cs.append(sl)
        ci.append(i0[a_lo:a_hi] * float(n_keys) + i1[0:n_b])
    cand_s = jnp.concatenate(cs, axis=0)
    cand_i = jnp.concatenate(ci, axis=0)
    n_cand = cand_s.shape[0]
    rid = lax.broadcasted_iota(jnp.int32, cand_s.shape, 0).astype(F32)
    best_s, best_i = [], []
    s = cand_s
    for _ in range(k_top):
        m = jnp.max(s, axis=0, keepdims=True)
        sel = jnp.min(jnp.where(s == m, rid, float(n_cand)), axis=0, keepdims=True)
        hit = rid == sel
        best_s.append(m)
        best_i.append(jnp.sum(jnp.where(hit, cand_i, 0.0), axis=0, keepdims=True))
        s = jnp.where(hit, -jnp.inf, s)
    bs = jnp.concatenate(best_s, axis=0)
    e = jnp.exp(bs - bs[0:1, :])
    gate = e / jnp.sum(e, axis=0, keepdims=True)
    return jnp.concatenate(best_i, axis=0).astype(jnp.int32), gate


def _peer_ffn_kernel(alpha, pitch, n_heads, h_ref, hn_ref, sc_ref, sh_ref, scn_ref, shn_ref, g2_ref,
                     lng_ref, lnb_ref, wq_ref, keys_ref, uv_hbm, o_ref, *scratch):
    slots = scratch[:EXPERT_SLOTS]
    (xin_scr, xnext_scr, f_scr, idxn_scr, gaten_scr, idxt_scr, gate_scr, idx_smem, sems,
     pub_sem) = scratch[EXPERT_SLOTS:]
    tb, d = h_ref.shape[1], h_ref.shape[2]
    n_sel = n_heads * PEER_TOPK
    lanes = slots[0].shape[-1]
    ru = d // lanes
    per_stage = n_sel // (2 * ru)
    step = pl.program_id(0) * pl.num_programs(1) + pl.program_id(1)
    n_steps = pl.num_programs(0) * pl.num_programs(1)
    cur = step % 2
    nxt = 1 - cur

    def route_into(xin, head):
        idx, gate = _route_head(xin, wq_ref, keys_ref, head)
        rows = pl.ds(pl.multiple_of(head * PEER_TOPK, PEER_TOPK), PEER_TOPK)
        idxn_scr[rows, :] = idx
        gaten_scr[rows, :] = gate

    def publish(buf):
        gate_scr[buf] = gaten_scr[...].T
        idxt_scr[...] = idxn_scr[...].astype(F32).T.astype(jnp.int32)
        rows = pl.ds(pl.multiple_of(buf * tb, tb), tb)
        copy = pltpu.make_async_copy(idxt_scr, idx_smem.at[rows, :], pub_sem.at[0])
        copy.start()
        copy.wait()

    def index_row(tt):
        return jnp.where(tt >= tb, nxt * tb + tt - tb, cur * tb + tt)

    def issue(row, slot, stage):
        for k in range(stage * per_stage, (stage + 1) * per_stage):
            e = idx_smem[row, k]
            src = uv_hbm.at[pl.ds(pl.multiple_of(e * ru, ru), ru), :]
            dst = slots[slot].at[pl.ds(k * pitch, ru), :]
            pltpu.make_async_copy(src, dst, sems.at[slot]).start(priority=k % 2)

    def wait(slot):
        total = n_sel * ru
        pltpu.make_async_copy(uv_hbm.at[pl.ds(0, total), :], slots[slot].at[pl.ds(0, total), :],
                              sems.at[slot]).wait()

    h = h_ref[0]
    xin_scr[...] = h * (1.0 + sc_ref[0]) + sh_ref[0]
    xnext_scr[...] = (hn_ref[0] * (1.0 + scn_ref[0]) + shn_ref[0]).astype(jnp.bfloat16)

    @pl.when(step == 0)
    def _():
        xin = xin_scr[...].astype(jnp.bfloat16)

        def first_route(head, carry):
            route_into(xin, head)
            return carry

        lax.fori_loop(0, n_heads, first_route, 0)
        publish(0)
        for t0 in range(PREFETCH_TOKENS):
            for stage in range(2 * ru):
                issue(t0, t0 % EXPERT_SLOTS, stage)

    eye = (lax.broadcasted_iota(jnp.int32, (n_sel, n_sel), 0)
           == lax.broadcasted_iota(jnp.int32, (n_sel, n_sel), 1))

    def token(t, slot):
        ahead = (slot + PREFETCH_TOKENS) % EXPERT_SLOTS
        ahead_row = index_row(t + PREFETCH_TOKENS)
        buf = slots[slot]
        wait(slot)
        x = xin_scr[pl.ds(t, 1), :]
        dots = None
        for s in range(ru):
            issue(ahead_row, ahead, s)
            words = buf[pl.ds(s, n_sel, stride=pitch), :]
            u_s = lax.bitcast_convert_type(words & U_HALF_MASK, F32)
            part = u_s * x[:, s * lanes:(s + 1) * lanes]
            dots = part if dots is None else dots + part
        g_row = gate_scr[cur, pl.ds(t, 1), :]
        g_col = jnp.sum(jnp.where(eye, g_row, 0.0), axis=1, keepdims=True)
        coef = g_col * _gelu(jnp.sum(dots, axis=1, keepdims=True))
        f_parts = []
        for s in range(ru):
            issue(ahead_row, ahead, ru + s)
            words = buf[pl.ds(s, n_sel, stride=pitch), :]
            v_s = lax.bitcast_convert_type(words << 16, F32)
            f_parts.append(jnp.sum(coef * v_s, axis=0, keepdims=True))
        f_scr[pl.ds(t, 1), :] = jnp.concatenate(f_parts, axis=1)

    def token_group(g, carry):
        for slot in range(EXPERT_SLOTS):
            token(g * EXPERT_SLOTS + slot, slot)
        return carry

    groups = tb // EXPERT_SLOTS
    n_phases = n_heads // ROUTE_HEADS
    groups_per_phase = (groups * 3 // 4) // n_phases

    def route_phase(phase, carry):
        lax.fori_loop(0, groups_per_phase, lambda g, c: token_group(phase * groups_per_phase + g, c), 0)
        xin = xnext_scr[...]
        for r in range(ROUTE_HEADS):
            route_into(xin, phase * ROUTE_HEADS + r)
        return carry

    lax.fori_loop(0, n_phases, route_phase, 0)
    publish(nxt)
    lax.fori_loop(n_phases * groups_per_phase, groups, token_group, 0)

    @pl.when(step == n_steps - 1)
    def _():
        for t0 in range(PREFETCH_TOKENS):
            wait((tb + t0) % EXPERT_SLOTS)

    o_ref[0] = _layer_norm(alpha * h + g2_ref[0] * f_scr[...], lng_ref[...], lnb_ref[...])


def _peer_block(h, sc2, sh2, g2, wq, keys, uv_tab, ln_g, ln_b, alpha):
    bsz, n, d = h.shape
    _, n_keys, half = keys.shape
    n_heads = wq.shape[1] // (2 * half)
    n_sel = n_heads * PEER_TOPK
    ru = d // V7X_LANES
    assert d % V7X_LANES == 0 and n_sel % (2 * ru) == 0
    pitch = _expert_pitch(ru)
    tb = _token_tile(n, 256)
    groups = tb // EXPERT_SLOTS
    n_phases = n_heads // ROUTE_HEADS
    assert tb % EXPERT_SLOTS == 0 and n_heads % ROUTE_HEADS == 0 and (groups * 3 // 4) // n_phases >= 1
    assert (groups - n_phases * ((groups * 3 // 4) // n_phases)) * EXPERT_SLOTS >= PREFETCH_TOKENS
    nb = n // tb
    last = bsz * nb - 1
    wq_heads = wq.astype(jnp.bfloat16).reshape(d, 2 * n_heads, half).transpose(1, 0, 2)

    def next_block(b, j):
        flat = jnp.minimum(b * nb + j + 1, last)
        return flat // nb, flat % nb

    tok = lambda b, j: (b, j, 0)
    tok_next = lambda b, j: next_block(b, j) + (0,)
    mod = lambda b, j: (b, 0, 0)
    mod_next = lambda b, j: (next_block(b, j)[0], 0, 0)
    full = lambda *shape: pl.BlockSpec(shape, lambda b, j: (0,) * len(shape))
    return pl.pallas_call(
        functools.partial(_peer_ffn_kernel, alpha, pitch, n_heads),
        grid=(bsz, nb),
        in_specs=[pl.BlockSpec((1, tb, d), tok), pl.BlockSpec((1, tb, d), tok_next),
                  pl.BlockSpec((1, 1, d), mod), pl.BlockSpec((1, 1, d), mod),
                  pl.BlockSpec((1, 1, d), mod_next), pl.BlockSpec((1, 1, d), mod_next),
                  pl.BlockSpec((1, 1, d), mod), full(1, d), full(1, d),
                  full(2 * n_heads, d, half), full(2, n_keys, half),
                  pl.BlockSpec(memory_space=pl.ANY)],
        out_specs=pl.BlockSpec((1, tb, d), tok),
        out_shape=jax.ShapeDtypeStruct((bsz, n, d), F32),
        scratch_shapes=[pltpu.VMEM((n_sel * pitch, V7X_LANES), jnp.int32)] * EXPERT_SLOTS
                       + [pltpu.VMEM((tb, d), F32), pltpu.VMEM((tb, d), jnp.bfloat16), pltpu.VMEM((tb, d), F32),
                          pltpu.VMEM((n_sel, tb), jnp.int32), pltpu.VMEM((n_sel, tb), F32),
                          pltpu.VMEM((tb, n_sel), jnp.int32),
                          pltpu.VMEM((2, tb, n_sel), F32), pltpu.SMEM((2 * tb, n_sel), jnp.int32),
                          pltpu.SemaphoreType.DMA((EXPERT_SLOTS,)), pltpu.SemaphoreType.DMA((1,))],
        compiler_params=_cparams("arbitrary", "arbitrary"),
        name="peer_ffn",
    )(h, h, sc2, sh2, sc2, sh2, g2, ln_g.reshape(1, d), ln_b.reshape(1, d), wq_heads, keys, uv_tab)


def kernel(x, c, ctx, c_ctx, ada_w, ada_b, ln_mix_g, ln_mix_b, ln_ffn_g, ln_ffn_b, even_w_in, even_w_out, ssm_lam_re, ssm_lam_im, ssm_log_step, ssm_b_re, ssm_b_im, ssm_c_re, ssm_c_im, ssm_d, glu_w, glu_b, conv_w, conv_b, pool_w, pool_scale, peer_wq, peer_keys, peer_u, peer_v):
    depth, d, _ = ada_w.shape
    bsz, seq, _ = x.shape
    ctx_len = ctx.shape[1]
    ssm_w = ssm_d.shape[-1]
    alpha = (2.0 * depth) ** 0.25

    pad = (-(bsz + 1)) % 8
    c_all = jnp.concatenate([c, c_ctx[None, :], jnp.zeros((pad, d), F32)], axis=0)
    mod = _ada_modulation(c_all, ada_w, ada_b)

    def six(rows):
        return [rows[:, None, k * d:(k + 1) * d] for k in range(6)]

    h_lat, h_ctx = x, ctx
    for layer in range(depth):
        ctx_out = any(j > layer and j % 2 == 0 for j in range(depth))
        even = layer % 2 == 0
        sh1, sc1, g1, sh2, sc2, g2 = six(mod[layer, :bsz])
        need_ctx = ctx_out or even
        if need_ctx:
            csh1, csc1, cg1, csh2, csc2, cg2 = six(jnp.broadcast_to(mod[layer, bsz:bsz + 1], (bsz, 6 * d)))
        lmg, lmb = ln_mix_g[layer], ln_mix_b[layer]
        if even:
            e = layer // 2
            ops = _s5_operators(ssm_lam_re[e], ssm_lam_im[e], ssm_log_step[e], ssm_b_re[e], ssm_b_im[e],
                                ssm_c_re[e], ssm_c_im[e])
            post = functools.partial(_even_post, w_out=even_w_out[e], glu_w=glu_w[e], glu_b=glu_b[e],
                                     conv_w=conv_w[e], conv_b=conv_b[e], ln_g=lmg, ln_b=lmb, alpha=alpha)
            ps_ctx, pc_ctx = _in_projection(h_ctx, csc1, csh1, even_w_in[e], ssm_w)
            y_ctx, finals = _s5_bidir(ps_ctx, ops, ssm_d[e], None)
            ps_lat, pc_lat = _in_projection(h_lat, sc1, sh1, even_w_in[e], ssm_w)
            y_lat, _ = _s5_bidir(ps_lat, ops, ssm_d[e], finals)
            h_lat = post(y_lat, pc_lat, h_lat, g1, seg_len=GRID_W)
            if ctx_out:
                h_ctx = post(y_ctx, pc_ctx, h_ctx, cg1, seg_len=ctx_len)
        else:
            o = layer // 2
            h_lat = _pool_mixer(h_lat, sc1, sh1, g1, pool_w[o], pool_scale[o], lmg, lmb, GRID_W, alpha)
            if ctx_out:
                h_ctx = _pool_mixer(h_ctx, csc1, csh1, cg1, pool_w[o], pool_scale[o], lmg, lmb, ctx_len, alpha)
        uv_tab = _expert_table(peer_u[layer], peer_v[layer])
        peer = functools.partial(_peer_block, wq=peer_wq[layer], keys=peer_keys[layer], uv_tab=uv_tab,
                                 ln_g=ln_ffn_g[layer], ln_b=ln_ffn_b[layer], alpha=alpha)
        h_lat = peer(h_lat, sc2, sh2, g2)
        if ctx_out:
            h_ctx = peer(h_ctx, csc2, csh2, cg2)
    return h_lat
```

```python
import functools
import math

import numpy as np
import jax
import jax.numpy as jnp
from jax import lax
from jax.experimental import pallas as pl
from jax.experimental.pallas import tpu as pltpu

F32 = jnp.float32
HI = lax.Precision.HIGHEST

GRID_W = 64
POOL_WINDOWS = (2, 4, 8, 16)
PEER_TOPK = 16
LN_EPS = 1e-5
STEP_CHUNK = 64
VMEM_LIMIT_V7X = 56 * 1024 * 1024
V7X_SUBLANES, V7X_LANES = 8, 128
NT_DIMS = (((1,), (1,)), ((), ()))


def _cparams(*sem):
    return pltpu.CompilerParams(dimension_semantics=sem, vmem_limit_bytes=VMEM_LIMIT_V7X)


def _dot(a, b):
    return jnp.dot(a, b, preferred_element_type=F32, precision=HI)


def _dot_bf16(a, w):
    return jnp.dot(a.astype(jnp.bfloat16), w, preferred_element_type=F32)


def _dot_3pass(a, b):
    def split(x):
        hi = x.astype(jnp.bfloat16)
        return hi, (x - hi.astype(F32)).astype(jnp.bfloat16)

    a_hi, a_lo = split(a)
    b_hi, b_lo = split(b)
    dot = functools.partial(jnp.dot, preferred_element_type=F32)
    return dot(a_hi, b_hi) + (dot(a_hi, b_lo) + dot(a_lo, b_hi))


def _dot_nt(a, b):
    return lax.dot_general(a, b, NT_DIMS, preferred_element_type=F32, precision=HI)


def _gelu(x):
    return 0.5 * x * (1.0 + lax.erf(x * (1.0 / math.sqrt(2.0))))


def _layer_norm(x, g, b):
    mu = jnp.mean(x, axis=-1, keepdims=True)
    xc = x - mu
    var = jnp.mean(xc * xc, axis=-1, keepdims=True)
    return xc * lax.rsqrt(var + LN_EPS) * g + b


def _token_tile(n, cap):
    t = min(n, cap)
    assert n % t == 0
    return t


def _ada_kernel(c_ref, w_ref, b_ref, o_ref):
    c = c_ref[...]
    s = c * jax.nn.sigmoid(c)
    o_ref[0] = _dot(s, w_ref[0]) + b_ref[0]


def _ada_modulation(c_all, ada_w, ada_b):
    depth, d, e = ada_w.shape
    rows = c_all.shape[0]
    te = e // 6
    return pl.pallas_call(
        _ada_kernel,
        grid=(depth, e // te),
        in_specs=[pl.BlockSpec((rows, d), lambda l, j: (0, 0)),
                  pl.BlockSpec((1, d, te), lambda l, j: (l, 0, j)),
                  pl.BlockSpec((1, 1, te), lambda l, j: (l, 0, j))],
        out_specs=pl.BlockSpec((1, rows, te), lambda l, j: (l, 0, j)),
        out_shape=jax.ShapeDtypeStruct((depth, rows, e), F32),
        compiler_params=_cparams("arbitrary", "arbitrary"),
        name="ada_modulation",
    )(c_all, ada_w, ada_b.reshape(depth, 1, e))


def _inproj_kernel(ssm_w, h_ref, sc_ref, sh_ref, w_ref, ps_ref, pc_ref):
    a = h_ref[0] * (1.0 + sc_ref[0]) + sh_ref[0]
    a = a.astype(jnp.bfloat16)
    ps_ref[0] = _dot_bf16(a, w_ref[:, :ssm_w])
    pc_ref[0] = _dot_bf16(a, w_ref[:, ssm_w:])


def _in_projection(h, sc, sh, w_in, ssm_w):
    bsz, n, d = h.shape
    e = w_in.shape[1]
    tl = _token_tile(n, 512)
    tok = lambda b, j: (b, j, 0)
    mod = lambda b, j: (b, 0, 0)
    return pl.pallas_call(
        functools.partial(_inproj_kernel, ssm_w),
        grid=(bsz, n // tl),
        in_specs=[pl.BlockSpec((1, tl, d), tok),
                  pl.BlockSpec((1, 1, d), mod),
                  pl.BlockSpec((1, 1, d), mod),
                  pl.BlockSpec((d, e), lambda b, j: (0, 0))],
        out_specs=[pl.BlockSpec((1, tl, ssm_w), tok),
                   pl.BlockSpec((1, tl, e - ssm_w), tok)],
        out_shape=[jax.ShapeDtypeStruct((bsz, n, ssm_w), F32),
                   jax.ShapeDtypeStruct((bsz, n, e - ssm_w), F32)],
        compiler_params=_cparams("arbitrary", "arbitrary"),
        name="even_in_projection",
    )(h, sc, sh, w_in.astype(jnp.bfloat16))


def _s5_param_kernel(chunk, lam_row_ref, lam_col_ref, ls_ref, bt_re_ref, bt_im_ref,
                     ct_re_ref, ct_im_ref, k_ref, w_re_ref, w_im_ref, g_re_ref, g_im_ref, a_ref):
    backward = pl.program_id(0) == 1
    p_dim = lam_row_ref.shape[-1]
    i_dim = k_ref.shape[-1]
    rows = chunk * i_dim
    step = jnp.exp(ls_ref[0, 0])
    lam_row = lam_row_ref[0, 0]
    lr_row, li_row = lam_row[0:1, :], lam_row[1:2, :]
    lam_col = lam_col_ref[0, 0]
    lr_col, li_col = lam_col[:, 0:1], lam_col[:, 1:2]
    ldr_row, ldi_row = lr_row * step, li_row * step
    ldr_col, ldi_col = lr_col * step, li_col * step

    def power_rows(expo):
        mag = jnp.exp(expo * ldr_row)
        return mag * jnp.cos(expo * ldi_row), mag * jnp.sin(expo * ldi_row)

    mag = jnp.exp(ldr_col)
    ab_re, ab_im = mag * jnp.cos(ldi_col), mag * jnp.sin(ldi_col)
    den = lr_col * lr_col + li_col * li_col
    nr = ab_re - 1.0
    f_re = (nr * lr_col + ab_im * li_col) / den
    f_im = (ab_im * lr_col - nr * li_col) / den
    bt_re, bt_im = bt_re_ref[0, 0], bt_im_ref[0, 0]
    bbt_re = f_re * bt_re - f_im * bt_im
    bbt_im = f_re * bt_im + f_im * bt_re
    bb_re, bb_im = bbt_re[:, :i_dim], bbt_im[:, :i_dim]

    ct_re, ct_im = ct_re_ref[0, 0], ct_im_ref[0, 0]
    tok = (lax.broadcasted_iota(jnp.int32, (rows, 1), 0) // i_dim).astype(F32)

    e_re, e_im = power_rows(tok)
    ck_re = ct_re * e_re - ct_im * e_im
    ck_im = ct_re * e_im + ct_im * e_re
    k_ref[0, 0] = _dot(ck_re, bb_re) - _dot(ck_im, bb_im)

    f_exp = jnp.where(backward, float(chunk) - tok, tok + 1.0)
    e_re, e_im = power_rows(f_exp)
    g_re_ref[0, 0] = ct_re * e_re - ct_im * e_im
    g_im_ref[0, 0] = ct_re * e_im + ct_im * e_re

    tok_l = (lax.broadcasted_iota(jnp.int32, (1, rows), 1) // i_dim).astype(F32)
    e_exp = jnp.where(backward, tok_l, float(chunk - 1) - tok_l)
    magw = jnp.exp(ldr_col * e_exp)
    ew_re, ew_im = magw * jnp.cos(ldi_col * e_exp), magw * jnp.sin(ldi_col * e_exp)
    w_re_ref[0, 0] = ew_re * bbt_re - ew_im * bbt_im
    w_im_ref[0, 0] = ew_re * bbt_im + ew_im * bbt_re

    full = jnp.full((1, 1), float(chunk), F32)
    a_re, a_im = power_rows(full)
    a_ref[0, 0] = jnp.concatenate([a_re, a_im], axis=0)


def _s5_operators(lam_re, lam_im, log_step, b_re, b_im, c_re, c_im):
    _, g, p = lam_re.shape
    i = b_re.shape[-1]
    t = STEP_CHUNK
    rows = t * i
    lam_row = jnp.stack([lam_re, lam_im], axis=2)
    lam_col = jnp.stack([lam_re, lam_im], axis=3)
    ls = log_step.reshape(2, g, 1, 1)
    bt_re = jnp.tile(b_re, (1, 1, 1, t))
    bt_im = jnp.tile(b_im, (1, 1, 1, t))
    ct_re = jnp.tile(c_re, (1, 1, t, 1))
    ct_im = jnp.tile(c_im, (1, 1, t, 1))
    blk = lambda *shape: pl.BlockSpec((1, 1) + shape, lambda dr, gi: (dr, gi, 0, 0))
    k, w_re, w_im, g_re, g_im, a = pl.pallas_call(
        functools.partial(_s5_param_kernel, t),
        grid=(2, g),
        in_specs=[blk(2, p), blk(p, 2), blk(1, 1), blk(p, rows), blk(p, rows),
                  blk(rows, p), blk(rows, p)],
        out_specs=[blk(rows, i), blk(p, rows), blk(p, rows), blk(rows, p), blk(rows, p), blk(2, p)],
        out_shape=[jax.ShapeDtypeStruct((2, g, rows, i), F32),
                   jax.ShapeDtypeStruct((2, g, p, rows), F32),
                   jax.ShapeDtypeStruct((2, g, p, rows), F32),
                   jax.ShapeDtypeStruct((2, g, rows, p), F32),
                   jax.ShapeDtypeStruct((2, g, rows, p), F32),
                   jax.ShapeDtypeStruct((2, g, 2, p), F32)],
        compiler_params=_cparams("arbitrary", "arbitrary"),
        name="s5_operators",
    )(lam_row, lam_col, ls, bt_re, bt_im, ct_re, ct_im)

    k5 = k.reshape(2, g, t, i, i)
    s_idx = np.arange(t)[:, None]
    t_idx = np.arange(t)[None, :]

    def toeplitz(kd, lag):
        m = kd[:, np.clip(lag, 0, t - 1)]
        m = jnp.where((lag >= 0)[None, :, :, None, None], m, 0.0)
        return m.transpose(0, 1, 4, 2, 3).reshape(g, rows, rows)

    m = jnp.stack([toeplitz(k5[0], t_idx - s_idx), toeplitz(k5[1], s_idx - t_idx)])
    return m, w_re, w_im, g_re, g_im, a


def _s5_scan_kernel(bsz, u_ref, d_ref, m_ref, w_re_ref, w_im_ref, g_re_ref, g_im_ref, a_ref,
                    h0_re_ref, h0_im_ref, y_ref, hf_re_ref, hf_im_ref,
                    z_re_scr, z_im_scr, hp_re_scr, hp_im_scr):
    backward = pl.program_id(1) == 1
    u = u_ref[0]
    n_chunks = u.shape[0] // bsz
    z_re_scr[...] = _dot_nt(u, w_re_ref[0, 0])
    z_im_scr[...] = _dot_nt(u, w_im_ref[0, 0])
    a = a_ref[0, 0]
    a_re, a_im = a[0:1, :], a[1:2, :]

    def chunk_step(k, carry):
        h_re, h_im = carry
        c = jnp.where(backward, n_chunks - 1 - k, k)
        rows = pl.ds(pl.multiple_of(c * bsz, bsz), bsz)
        hp_re_scr[rows, :] = h_re
        hp_im_scr[rows, :] = h_im
        n_re = a_re * h_re - a_im * h_im + z_re_scr[rows, :]
        n_im = a_re * h_im + a_im * h_re + z_im_scr[rows, :]
        return n_re, n_im

    h_re, h_im = lax.fori_loop(0, n_chunks, chunk_step, (h0_re_ref[0, 0], h0_im_ref[0, 0]))
    hf_re_ref[0, 0] = h_re
    hf_im_ref[0, 0] = h_im

    y = (_dot_3pass(u, m_ref[0, 0]) + _dot_nt(hp_re_scr[...], g_re_ref[0, 0])
         - _dot_nt(hp_im_scr[...], g_im_ref[0, 0]))

    @pl.when(jnp.logical_not(backward))
    def _():
        y_ref[0] = d_ref[0] * u + y

    @pl.when(backward)
    def _():
        y_ref[0] = y_ref[0] + y


def _s5_bidir(u, ops, d_skip, h0):
    m, w_re, w_im, g_re, g_im, a = ops
    bsz, n, width = u.shape
    _, g, p, rows = w_re.shape
    t = STEP_CHUNK
    i = rows // t
    nc = n // t
    r = nc * bsz
    ug = u.reshape(bsz, nc, t, g, i).transpose(3, 1, 0, 2, 4).reshape(g, r, rows)
    dg = jnp.tile(d_skip.reshape(g, 1, i), (1, 1, t))
    if h0 is None:
        h0 = (jnp.zeros((2, g, bsz, p), F32),) * 2
    per_g = lambda *shape: pl.BlockSpec((1,) + shape, lambda gi, dr: (gi, 0, 0))
    per_dg = lambda *shape: pl.BlockSpec((1, 1) + shape, lambda gi, dr: (dr, gi, 0, 0))
    y, hf_re, hf_im = pl.pallas_call(
        functools.partial(_s5_scan_kernel, bsz),
        grid=(g, 2),
        in_specs=[per_g(r, rows), per_g(1, rows), per_dg(rows, rows), per_dg(p, rows), per_dg(p, rows),
                  per_dg(rows, p), per_dg(rows, p), per_dg(2, p), per_dg(bsz, p), per_dg(bsz, p)],
        out_specs=[per_g(r, rows), per_dg(bsz, p), per_dg(bsz, p)],
        out_shape=[jax.ShapeDtypeStruct((g, r, rows), F32),
                   jax.ShapeDtypeStruct((2, g, bsz, p), F32),
                   jax.ShapeDtypeStruct((2, g, bsz, p), F32)],
        scratch_shapes=[pltpu.VMEM((r, p), F32)] * 4,
        compiler_params=_cparams("arbitrary", "arbitrary"),
        name="s5_scan",
    )(ug, dg, m, w_re, w_im, g_re, g_im, a, h0[0], h0[1])
    y = y.reshape(g, nc, bsz, t, i).transpose(2, 1, 3, 0, 4).reshape(bsz, n, width)
    return y, (hf_re, hf_im)


def _even_post_kernel(seg_len, alpha, ys_ref, pc_ref, h_ref, g1_ref, wout_ref, gluw_ref, glub_ref,
                      cw_ref, cb_ref, lng_ref, lnb_ref, o_ref):
    ys = ys_ref[0]
    tl, sw = ys.shape
    g = _gelu(ys)
    ya = g * jax.nn.sigmoid(_dot_bf16(g, gluw_ref[...]) + glub_ref[...])
    cwid = pc_ref.shape[-1] // 3
    z = pc_ref[0, :, cwid:2 * cwid] * pc_ref[0, :, 2 * cwid:]
    pos = lax.broadcasted_iota(jnp.int32, (tl, 1), 0) % seg_len
    z_prev = jnp.where(pos == 0, 0.0, pltpu.roll(z, 1, axis=0))
    z_next = jnp.where(pos == seg_len - 1, 0.0, pltpu.roll(z, tl - 1, axis=0))
    cw = cw_ref[...]
    conv = cw[0:1, :] * z_prev + cw[1:2, :] * z + cw[2:3, :] * z_next + cb_ref[...]
    yb = pc_ref[0, :, :cwid] * conv
    out = _dot_bf16(ya, wout_ref[:sw, :]) + _dot_bf16(yb, wout_ref[sw:, :])
    h = h_ref[0]
    o_ref[0] = _layer_norm(alpha * h + g1_ref[0] * out, lng_ref[...], lnb_ref[...])


def _even_post(y_ssm, p_conv, h, g1, w_out, glu_w, glu_b, conv_w, conv_b, ln_g, ln_b, seg_len, alpha):
    bsz, n, d = h.shape
    sw = y_ssm.shape[-1]
    cw3 = p_conv.shape[-1]
    cwid = cw3 // 3
    tl = _token_tile(n, 512)
    assert tl % seg_len == 0
    tok = lambda b, j: (b, j, 0)
    mod = lambda b, j: (b, 0, 0)
    full = lambda *shape: pl.BlockSpec(shape, lambda b, j: (0,) * len(shape))
    return pl.pallas_call(
        functools.partial(_even_post_kernel, seg_len, alpha),
        grid=(bsz, n // tl),
        in_specs=[pl.BlockSpec((1, tl, sw), tok), pl.BlockSpec((1, tl, cw3), tok),
                  pl.BlockSpec((1, tl, d), tok), pl.BlockSpec((1, 1, d), mod),
                  full(sw + cwid, d), full(sw, sw), full(1, sw), full(3, cwid), full(1, cwid),
                  full(1, d), full(1, d)],
        out_specs=pl.BlockSpec((1, tl, d), tok),
        out_shape=jax.ShapeDtypeStruct((bsz, n, d), F32),
        compiler_params=_cparams("arbitrary", "arbitrary"),
        name="even_post",
    )(y_ssm, p_conv, h, g1, w_out.astype(jnp.bfloat16), glu_w.astype(jnp.bfloat16), glu_b.reshape(1, sw), conv_w, conv_b.reshape(1, cwid),
      ln_g.reshape(1, d), ln_b.reshape(1, d))


def _pool_matrices(seg_len, tile):
    pos = np.arange(tile) % seg_len
    seg = np.arange(tile) // seg_len
    mats = []
    for w in POOL_WINDOWS:
        lo = np.clip(pos - w // 2, 0, seg_len - 1)
        hi = np.clip(pos + w // 2 - 1, 0, seg_len - 1)
        cnt = (hi - lo + 1).astype(np.float64)
        inside = ((seg[:, None] == seg[None, :]) & (pos[None, :] >= lo[:, None])
                  & (pos[None, :] <= hi[:, None]))
        mats.append(inside / cnt[:, None] - np.eye(tile))
    return np.stack(mats).astype(np.float32)


def _pool_kernel(alpha, h_ref, sc_ref, sh_ref, g1_ref, pm_ref, pw_ref, ps_ref, lng_ref, lnb_ref, o_ref):
    h = h_ref[0]
    a = h * (1.0 + sc_ref[0]) + sh_ref[0]
    n_win, pg, _ = pw_ref.shape
    ys = []
    for gi in range(n_win):
        z = _dot(pm_ref[gi], a[:, gi * pg:(gi + 1) * pg])
        ys.append(_dot_bf16(z, pw_ref[gi]))
    y = jnp.concatenate(ys, axis=-1) * ps_ref[...]
    o_ref[0] = _layer_norm(alpha * h + g1_ref[0] * y, lng_ref[...], lnb_ref[...])


def _pool_mixer(h, sc, sh, g1, pool_w, pool_scale, ln_g, ln_b, seg_len, alpha):
    bsz, n, d = h.shape
    tl = _token_tile(n, 256)
    assert tl % seg_len == 0
    pm = jnp.asarray(_pool_matrices(seg_len, tl))
    n_win, pg, _ = pool_w.shape
    tok = lambda b, j: (b, j, 0)
    mod = lambda b, j: (b, 0, 0)
    full = lambda *shape: pl.BlockSpec(shape, lambda b, j: (0,) * len(shape))
    return pl.pallas_call(
        functools.partial(_pool_kernel, alpha),
        grid=(bsz, n // tl),
        in_specs=[pl.BlockSpec((1, tl, d), tok), pl.BlockSpec((1, 1, d), mod), pl.BlockSpec((1, 1, d), mod),
                  pl.BlockSpec((1, 1, d), mod), full(n_win, tl, tl), full(n_win, pg, pg), full(1, d),
                  full(1, d), full(1, d)],
        out_specs=pl.BlockSpec((1, tl, d), tok),
        out_shape=jax.ShapeDtypeStruct((bsz, n, d), F32),
        compiler_params=_cparams("arbitrary", "arbitrary"),
        name="pool_mixer",
    )(h, sc, sh, g1, pm, pool_w.astype(jnp.bfloat16), pool_scale.reshape(1, d), ln_g.reshape(1, d), ln_b.reshape(1, d))


def _top_rows(s, n_top):
    n = s.shape[0]
    rid = lax.broadcasted_iota(jnp.int32, s.shape, 0).astype(F32)
    vals, rows = [], []
    for _ in range(n_top):
        m = jnp.max(s, axis=0, keepdims=True)
        sel = jnp.min(jnp.where(s == m, rid, float(n)), axis=0, keepdims=True)
        vals.append(m)
        rows.append(sel)
        s = jnp.where(rid == sel, -jnp.inf, s)
    return jnp.concatenate(vals, axis=0), jnp.concatenate(rows, axis=0)


def _candidate_slabs(k_top):
    slabs = []
    for a in range(k_top):
        n_valid = k_top // (a + 1)
        if n_valid == 1:
            slabs.append((a, k_top, 1, 1))
            break
        slabs.append((a, a + 1, -(-n_valid // 8) * 8, n_valid))
    return slabs


U_HALF_MASK = -65536
ROUTE_HEADS = 2
EXPERT_SLOTS = 16
PREFETCH_TOKENS = EXPERT_SLOTS - 1


def _expert_pitch(rows):
    tiles = -(-rows // V7X_SUBLANES)
    return V7X_SUBLANES * (tiles if tiles % 2 else tiles + 1)


def _expert_table(u_tab, v_tab):
    n_exp, d = u_tab.shape

    def bits(tab):
        return lax.bitcast_convert_type(tab.astype(jnp.bfloat16), jnp.uint16).astype(jnp.uint32)

    words = lax.bitcast_convert_type((bits(u_tab) << 16) | bits(v_tab), jnp.int32)
    return words.reshape(n_exp * d // V7X_LANES, V7X_LANES)


def _route_head(xin, wq_ref, keys_ref, head):
    n_keys = keys_ref.shape[1]
    k_top = PEER_TOPK
    tops = []
    for side in range(2):
        q = jnp.dot(xin, wq_ref[head * 2 + side], preferred_element_type=F32)
        tops.append(_top_rows(_dot_nt(keys_ref[side], q), k_top))
    (v0, i0), (v1, i1) = tops
    cs, ci = [], []
    for a_lo, a_hi, n_b, n_valid in _candidate_slabs(k_top):
        sl = v0[a_lo:a_hi] + v1[0:n_b]
        if n_valid < n_b:
            sl = jnp.where(lax.broadcasted_iota(jnp.int32, sl.shape, 0) < n_valid, sl, -jnp.inf)
        cs.append(sl)
        ci.append(i0[a_lo:a_hi] * float(n_keys) + i1[0:n_b])
    cand_s = jnp.concatenate(cs, axis=0)
    cand_i = jnp.concatenate(ci, axis=0)
    n_cand = cand_s.shape[0]
    rid = lax.broadcasted_iota(jnp.int32, cand_s.shape, 0).astype(F32)
    best_s, best_i = [], []
    s = cand_s
    for _ in range(k_top):
        m = jnp.max(s, axis=0, keepdims=True)
        sel = jnp.min(jnp.where(s == m, rid, float(n_cand)), axis=0, keepdims=True)
        hit = rid == sel
        best_s.append(m)
        best_i.append(jnp.sum(jnp.where(hit, cand_i, 0.0), axis=0, keepdims=True))
        s = jnp.where(hit, -jnp.inf, s)
    bs = jnp.concatenate(best_s, axis=0)
    e = jnp.exp(bs - bs[0:1, :])
    gate = e / jnp.sum(e, axis=0, keepdims=True)
    return jnp.concatenate(best_i, axis=0).astype(jnp.int32), gate


def _peer_ffn_kernel(alpha, pitch, n_heads, h_ref, hn_ref, sc_ref, sh_ref, scn_ref, shn_ref, g2_ref,
                     lng_ref, lnb_ref, wq_ref, keys_ref, uv_hbm, o_ref, *scratch):
    slots = scratch[:EXPERT_SLOTS]
    (xin_scr, xnext_scr, f_scr, idxn_scr, gaten_scr, idxt_scr, gate_scr, idx_smem, sems,
     pub_sem) = scratch[EXPERT_SLOTS:]
    tb, d = h_ref.shape[1], h_ref.shape[2]
    n_sel = n_heads * PEER_TOPK
    lanes = slots[0].shape[-1]
    ru = d // lanes
    per_stage = n_sel // (2 * ru)
    step = pl.program_id(0) * pl.num_programs(1) + pl.program_id(1)
    n_steps = pl.num_programs(0) * pl.num_programs(1)
    cur = step % 2
    nxt = 1 - cur

    def route_into(xin, head):
        idx, gate = _route_head(xin, wq_ref, keys_ref, head)
        rows = pl.ds(pl.multiple_of(head * PEER_TOPK, PEER_TOPK), PEER_TOPK)
        idxn_scr[rows, :] = idx
        gaten_scr[rows, :] = gate

    def publish(buf):
        gate_scr[buf] = gaten_scr[...].T
        idxt_scr[...] = idxn_scr[...].astype(F32).T.astype(jnp.int32)
        rows = pl.ds(pl.multiple_of(buf * tb, tb), tb)
        copy = pltpu.make_async_copy(idxt_scr, idx_smem.at[rows, :], pub_sem.at[0])
        copy.start()
        copy.wait()

    def index_row(tt):
        return jnp.where(tt >= tb, nxt * tb + tt - tb, cur * tb + tt)

    def issue(row, slot, stage):
        for k in range(stage * per_stage, (stage + 1) * per_stage):
            e = idx_smem[row, k]
            src = uv_hbm.at[pl.ds(pl.multiple_of(e * ru, ru), ru), :]
            dst = slots[slot].at[pl.ds(k * pitch, ru), :]
            pltpu.make_async_copy(src, dst, sems.at[slot]).start(priority=k % 2)

    def wait(slot):
        total = n_sel * ru
        pltpu.make_async_copy(uv_hbm.at[pl.ds(0, total), :], slots[slot].at[pl.ds(0, total), :],
                              sems.at[slot]).wait()

    h = h_ref[0]
    xin_scr[...] = h * (1.0 + sc_ref[0]) + sh_ref[0]
    xnext_scr[...] = (hn_ref[0] * (1.0 + scn_ref[0]) + shn_ref[0]).astype(jnp.bfloat16)

    @pl.when(step == 0)
    def _():
        xin = xin_scr[...].astype(jnp.bfloat16)

        def first_route(head, carry):
            route_into(xin, head)
            return carry

        lax.fori_loop(0, n_heads, first_route, 0)
        publish(0)
        for t0 in range(PREFETCH_TOKENS):
            for stage in range(2 * ru):
                issue(t0, t0 % EXPERT_SLOTS, stage)

    eye = (lax.broadcasted_iota(jnp.int32, (n_sel, n_sel), 0)
           == lax.broadcasted_iota(jnp.int32, (n_sel, n_sel), 1))

    def token(t, slot):
        ahead = (slot + PREFETCH_TOKENS) % EXPERT_SLOTS
        ahead_row = index_row(t + PREFETCH_TOKENS)
        buf = slots[slot]
        wait(slot)
        x = xin_scr[pl.ds(t, 1), :]
        dots = None
        for s in range(ru):
            issue(ahead_row, ahead, s)
            words = buf[pl.ds(s, n_sel, stride=pitch), :]
            u_s = lax.bitcast_convert_type(words & U_HALF_MASK, F32)
            part = u_s * x[:, s * lanes:(s + 1) * lanes]
            dots = part if dots is None else dots + part
        g_row = gate_scr[cur, pl.ds(t, 1), :]
        g_col = jnp.sum(jnp.where(eye, g_row, 0.0), axis=1, keepdims=True)
        coef = g_col * _gelu(jnp.sum(dots, axis=1, keepdims=True))
        f_parts = []
        for s in range(ru):
            issue(ahead_row, ahead, ru + s)
            words = buf[pl.ds(s, n_sel, stride=pitch), :]
            v_s = lax.bitcast_convert_type(words << 16, F32)
            f_parts.append(jnp.sum(coef * v_s, axis=0, keepdims=True))
        f_scr[pl.ds(t, 1), :] = jnp.concatenate(f_parts, axis=1)

    def token_group(g, carry):
        for slot in range(EXPERT_SLOTS):
            token(g * EXPERT_SLOTS + slot, slot)
        return carry

    groups = tb // EXPERT_SLOTS
    n_phases = n_heads // ROUTE_HEADS
    groups_per_phase = (groups * 3 // 4) // n_phases

    def route_phase(phase, carry):
        lax.fori_loop(0, groups_per_phase, lambda g, c: token_group(phase * groups_per_phase + g, c), 0)
        xin = xnext_scr[...]
        for r in range(ROUTE_HEADS):
            route_into(xin, phase * ROUTE_HEADS + r)
        return carry

    lax.fori_loop(0, n_phases, route_phase, 0)
    publish(nxt)
    lax.fori_loop(n_phases * groups_per_phase, groups, token_group, 0)

    @pl.when(step == n_steps - 1)
    def _():
        for t0 in range(PREFETCH_TOKENS):
            wait((tb + t0) % EXPERT_SLOTS)

    o_ref[0] = _layer_norm(alpha * h + g2_ref[0] * f_scr[...], lng_ref[...], lnb_ref[...])


def _peer_block(h, sc2, sh2, g2, wq, keys, uv_tab, ln_g, ln_b, alpha):
    bsz, n, d = h.shape
    _, n_keys, half = keys.shape
    n_heads = wq.shape[1] // (2 * half)
    n_sel = n_heads * PEER_TOPK
    ru = d // V7X_LANES
    assert d % V7X_LANES == 0 and n_sel % (2 * ru) == 0
    pitch = _expert_pitch(ru)
    tb = _token_tile(n, 256)
    groups = tb // EXPERT_SLOTS
    n_phases = n_heads // ROUTE_HEADS
    assert tb % EXPERT_SLOTS == 0 and n_heads % ROUTE_HEADS == 0 and (groups * 3 // 4) // n_phases >= 1
    assert (groups - n_phases * ((groups * 3 // 4) // n_phases)) * EXPERT_SLOTS >= PREFETCH_TOKENS
    nb = n // tb
    last = bsz * nb - 1
    wq_heads = wq.astype(jnp.bfloat16).reshape(d, 2 * n_heads, half).transpose(1, 0, 2)

    def next_block(b, j):
        flat = jnp.minimum(b * nb + j + 1, last)
        return flat // nb, flat % nb

    tok = lambda b, j: (b, j, 0)
    tok_next = lambda b, j: next_block(b, j) + (0,)
    mod = lambda b, j: (b, 0, 0)
    mod_next = lambda b, j: (next_block(b, j)[0], 0, 0)
    full = lambda *shape: pl.BlockSpec(shape, lambda b, j: (0,) * len(shape))
    return pl.pallas_call(
        functools.partial(_peer_ffn_kernel, alpha, pitch, n_heads),
        grid=(bsz, nb),
        in_specs=[pl.BlockSpec((1, tb, d), tok), pl.BlockSpec((1, tb, d), tok_next),
                  pl.BlockSpec((1, 1, d), mod), pl.BlockSpec((1, 1, d), mod),
                  pl.BlockSpec((1, 1, d), mod_next), pl.BlockSpec((1, 1, d), mod_next),
                  pl.BlockSpec((1, 1, d), mod), full(1, d), full(1, d),
                  full(2 * n_heads, d, half), full(2, n_keys, half),
                  pl.BlockSpec(memory_space=pl.ANY)],
        out_specs=pl.BlockSpec((1, tb, d), tok),
        out_shape=jax.ShapeDtypeStruct((bsz, n, d), F32),
        scratch_shapes=[pltpu.VMEM((n_sel * pitch, V7X_LANES), jnp.int32)] * EXPERT_SLOTS
                       + [pltpu.VMEM((tb, d), F32), pltpu.VMEM((tb, d), jnp.bfloat16), pltpu.VMEM((tb, d), F32),
                          pltpu.VMEM((n_sel, tb), jnp.int32), pltpu.VMEM((n_sel, tb), F32),
                          pltpu.VMEM((tb, n_sel), jnp.int32),
                          pltpu.VMEM((2, tb, n_sel), F32), pltpu.SMEM((2 * tb, n_sel), jnp.int32),
                          pltpu.SemaphoreType.DMA((EXPERT_SLOTS,)), pltpu.SemaphoreType.DMA((1,))],
        compiler_params=_cparams("arbitrary", "arbitrary"),
        name="peer_ffn",
    )(h, h, sc2, sh2, sc2, sh2, g2, ln_g.reshape(1, d), ln_b.reshape(1, d), wq_heads, keys, uv_tab)


def kernel(x, c, ctx, c_ctx, ada_w, ada_b, ln_mix_g, ln_mix_b, ln_ffn_g, ln_ffn_b, even_w_in, even_w_out, ssm_lam_re, ssm_lam_im, ssm_log_step, ssm_b_re, ssm_b_im, ssm_c_re, ssm_c_im, ssm_d, glu_w, glu_b, conv_w, conv_b, pool_w, pool_scale, peer_wq, peer_keys, peer_u, peer_v):
    depth, d, _ = ada_w.shape
    bsz, seq, _ = x.shape
    ctx_len = ctx.shape[1]
    ssm_w = ssm_d.shape[-1]
    alpha = (2.0 * depth) ** 0.25

    pad = (-(bsz + 1)) % 8
    c_all = jnp.concatenate([c, c_ctx[None, :], jnp.zeros((pad, d), F32)], axis=0)
    mod = _ada_modulation(c_all, ada_w, ada_b)

    def six(rows):
        return [rows[:, None, k * d:(k + 1) * d] for k in range(6)]

    h_lat, h_ctx = x, ctx
    for layer in range(depth):
        ctx_out = any(j > layer and j % 2 == 0 for j in range(depth))
        even = layer % 2 == 0
        sh1, sc1, g1, sh2, sc2, g2 = six(mod[layer, :bsz])
        need_ctx = ctx_out or even
        if need_ctx:
            csh1, csc1, cg1, csh2, csc2, cg2 = six(jnp.broadcast_to(mod[layer, bsz:bsz + 1], (bsz, 6 * d)))
        lmg, lmb = ln_mix_g[layer], ln_mix_b[layer]
        if even:
            e = layer // 2
            ops = _s5_operators(ssm_lam_re[e], ssm_lam_im[e], ssm_log_step[e], ssm_b_re[e], ssm_b_im[e],
                                ssm_c_re[e], ssm_c_im[e])
            post = functools.partial(_even_post, w_out=even_w_out[e], glu_w=glu_w[e], glu_b=glu_b[e],
                                     conv_w=conv_w[e], conv_b=conv_b[e], ln_g=lmg, ln_b=lmb, alpha=alpha)
            ps_ctx, pc_ctx = _in_projection(h_ctx, csc1, csh1, even_w_in[e], ssm_w)
            y_ctx, finals = _s5_bidir(ps_ctx, ops, ssm_d[e], None)
            ps_lat, pc_lat = _in_projection(h_lat, sc1, sh1, even_w_in[e], ssm_w)
            y_lat, _ = _s5_bidir(ps_lat, ops, ssm_d[e], finals)
            h_lat = post(y_lat, pc_lat, h_lat, g1, seg_len=GRID_W)
            if ctx_out:
                h_ctx = post(y_ctx, pc_ctx, h_ctx, cg1, seg_len=ctx_len)
        else:
            o = layer // 2
            h_lat = _pool_mixer(h_lat, sc1, sh1, g1, pool_w[o], pool_scale[o], lmg, lmb, GRID_W, alpha)
            if ctx_out:
                h_ctx = _pool_mixer(h_ctx, csc1, csh1, cg1, pool_w[o], pool_scale[o], lmg, lmb, ctx_len, alpha)
        uv_tab = _expert_table(peer_u[layer], peer_v[layer])
        peer = functools.partial(_peer_block, wq=peer_wq[layer], keys=peer_keys[layer], uv_tab=uv_tab,
                                 ln_g=ln_ffn_g[layer], ln_b=ln_ffn_b[layer], alpha=alpha)
        h_lat = peer(h_lat, sc2, sh2, g2)
        if ctx_out:
            h_ctx = peer(h_ctx, csc2, csh2, cg2)
    return h_lat
```
